```python
import jax, jax.numpy as jnp
from jax import lax
import numpy as np

D_MODEL = 1024
BATCH = 32
SEQ = 2048
DEPTH = 1
DEC_BATCH = 8
DEC_SEQ = 32
PAST_LEN = 1024

CHUNK = 64
N_META = 16
MIX_WIDTH = D_MODEL
HG_WIDTH = MIX_WIDTH // 2
HG_HEADS = 4
HG_DIM = HG_WIDTH // HG_HEADS
ATT_WIDTH = MIX_WIDTH - HG_WIDTH
ATT_HEAD_DIM = 64
N_Q_HEADS = ATT_WIDTH // ATT_HEAD_DIM
N_KV_HEADS = 2
Q_PER_KV = N_Q_HEADS // N_KV_HEADS
KV_WIDTH = N_KV_HEADS * ATT_HEAD_DIM
WINDOW = 128
WIN_CHUNKS = WINDOW // CHUNK
ROPE_THETA = 10000.0
D_FF = -(-8 * D_MODEL // (3 * 256)) * 256
IN_COLS = 4 * HG_WIDTH + ATT_WIDTH + 2 * KV_WIDTH
EPS = 1e-6

kernel_name = 'hymba_hgrn2_swa_sink_streaming_step'


def rmsnorm(x, g):
    xf = x.astype(jnp.float32)
    y = xf * lax.rsqrt(jnp.mean(xf * xf, axis=-1, keepdims=True) + EPS)
    return y.astype(x.dtype) * g.astype(x.dtype)


def rope(x, pos):
    inv = ROPE_THETA ** (-jnp.arange(0, ATT_HEAD_DIM, 2, dtype=jnp.float32) / ATT_HEAD_DIM)
    ang = pos.astype(jnp.float32)[:, None] * inv[None, :]
    ang = jnp.concatenate([ang, ang], axis=-1)[:, None, :]
    xf = x.astype(jnp.float32)
    x1, x2 = jnp.split(xf, 2, axis=-1)
    rot = jnp.concatenate([-x2, x1], axis=-1)
    return (xf * jnp.cos(ang) + rot * jnp.sin(ang)).astype(x.dtype)


def project(hn, w, lb, pos):
    lead = hn.shape[:-1]
    idx = [HG_WIDTH, 2 * HG_WIDTH, 3 * HG_WIDTH, 4 * HG_WIDTH,
           4 * HG_WIDTH + ATT_WIDTH, 4 * HG_WIDTH + ATT_WIDTH + KV_WIDTH]
    hq, hf, hi, hg, aq, ak, av = jnp.split(hn @ w, idx, axis=-1)
    heads = lambda t: t.reshape(*lead, HG_HEADS, HG_DIM)
    fgate = lb + (1.0 - lb) * jax.nn.sigmoid(hf.astype(jnp.float32))
    q_h = heads(jax.nn.silu(hq.astype(jnp.float32)))
    logf = heads(jnp.log(fgate))
    k_h = heads(1.0 - fgate)
    v_h = heads(hi)
    g_h = heads(hg)
    aq = rope(aq.reshape(*lead, N_Q_HEADS, ATT_HEAD_DIM), pos).reshape(*lead, N_KV_HEADS, Q_PER_KV, ATT_HEAD_DIM)
    ak = rope(ak.reshape(*lead, N_KV_HEADS, ATT_HEAD_DIM), pos)
    av = av.reshape(*lead, N_KV_HEADS, ATT_HEAD_DIM)
    return q_h, logf, k_h, v_h, g_h, aq, ak, av


def hgrn_block(q, logf, k, v, S0):
    q, k, v = (t.astype(jnp.float32) for t in (q, k, v))
    L = q.shape[1]
    b = jnp.cumsum(logf.astype(jnp.float32), axis=1)
    o_inter = jnp.einsum('blhk,bhkv->blhv', q * jnp.exp(b), S0)
    causal = jnp.tril(jnp.ones((L, L), dtype=bool))
    diff = jnp.minimum(b[:, :, None] - b[:, None, :], 0.0)
    decay = jnp.where(causal[None, :, :, None, None], jnp.exp(diff), 0.0)
    A = jnp.einsum('bthk,btshk,bshk->bhts', q, decay, k)
    o_intra = jnp.einsum('bhts,bshv->bthv', A, v)
    bL = b[:, -1]
    S_new = jnp.exp(bL)[..., None] * S0 + jnp.einsum('bshk,bshv->bhkv', k * jnp.exp(bL[:, None] - b), v)
    return o_inter + o_intra, S_new


def hgrn_readout(o, g, gain):
    on = o * lax.rsqrt(jnp.mean(o * o, axis=-1, keepdims=True) + EPS)
    y = on * gain.astype(jnp.float32).reshape(HG_HEADS, HG_DIM) * jax.nn.silu(g.astype(jnp.float32))
    return y.reshape(*o.shape[:-2], HG_WIDTH)


def sink_attention(q, k, v, sinks, valid):
    s = jnp.einsum('...qgrd,...kgd->...grqk', q, k).astype(jnp.float32) * (ATT_HEAD_DIM ** -0.5)
    if valid is not None:
        s = jnp.where(valid[..., None, None, None, :], s, jnp.finfo(jnp.float32).min)
    sink = jnp.broadcast_to(sinks.astype(jnp.float32)[:, :, None, None], s.shape[:-1] + (1,))
    p = jax.nn.softmax(jnp.concatenate([s, sink], axis=-1), axis=-1)[..., :-1]
    return jnp.einsum('...grqk,...kgd->...qgrd', p.astype(v.dtype), v)


def finish_layer(h, o_hg, g_hg, o_att, hg_norm_l, attn_norm_l, w_out_l, norm2_l, w_ffn_in_l, w_ffn_out_l):
    mixed = jnp.concatenate([hgrn_readout(o_hg, g_hg, hg_norm_l).astype(h.dtype),
                             rmsnorm(o_att, attn_norm_l)], axis=-1) @ w_out_l
    h = h + mixed
    gate, up = jnp.split(rmsnorm(h, norm2_l) @ w_ffn_in_l, 2, axis=-1)
    return h + (jax.nn.silu(gate) * up) @ w_ffn_out_l


def band(t, nc):
    B = t.shape[0]
    pad = jnp.zeros((B, WIN_CHUNKS * CHUNK) + t.shape[2:], t.dtype)
    tp = jnp.concatenate([pad, t], axis=1).reshape(B, nc + WIN_CHUNKS, CHUNK, *t.shape[2:])
    return jnp.concatenate([tp[:, j:j + nc] for j in range(WIN_CHUNKS + 1)], axis=2)


def setup_inputs(seed: int = 0) -> dict:
    key = jax.random.key(seed)
    ks = jax.random.split(key, 24)
    n = lambda i, shape, s=1.0: jax.random.normal(ks[i], shape, jnp.float32) * s
    win_keep = min(WINDOW, PAST_LEN)
    kv_shape = lambda L: (DEPTH, DEC_BATCH, L, N_KV_HEADS, ATT_HEAD_DIM)
    return {
        'x_prompt': n(0, (BATCH, SEQ, D_MODEL)),
        'x_sample': n(1, (DEC_BATCH, DEC_SEQ, D_MODEL)),
        'cache_meta_k': n(2, kv_shape(N_META)),
        'cache_meta_v': n(3, kv_shape(N_META)),
        'cache_win_k': n(4, kv_shape(win_keep)),
        'cache_win_v': n(5, kv_shape(win_keep)),
        'state_hgrn': n(6, (DEPTH, DEC_BATCH, HG_HEADS, HG_DIM, HG_DIM), 0.3),
        'meta_tokens': n(7, (N_META, D_MODEL)),
        'norm1': 1.0 + n(8, (DEPTH, D_MODEL), 0.02),
        'w_in': n(9, (DEPTH, D_MODEL, IN_COLS), D_MODEL ** -0.5),
        'lb_param': n(10, (DEPTH + 1, HG_WIDTH), 0.1),
        'hg_norm': 1.0 + n(11, (DEPTH, HG_WIDTH), 0.02),
        'attn_sinks': n(12, (DEPTH, N_KV_HEADS, Q_PER_KV), 0.5),
        'attn_norm': 1.0 + n(13, (DEPTH, ATT_WIDTH), 0.02),
        'w_out': n(14, (DEPTH, MIX_WIDTH, D_MODEL), MIX_WIDTH ** -0.5),
        'norm2': 1.0 + n(15, (DEPTH, D_MODEL), 0.02),
        'w_ffn_in': n(16, (DEPTH, D_MODEL, 2 * D_FF), D_MODEL ** -0.5),
        'w_ffn_out': n(17, (DEPTH, D_FF, D_MODEL), D_FF ** -0.5),
        'final_norm': 1.0 + n(18, (D_MODEL,), 0.02),
    }


def reference(x_prompt, x_sample, cache_meta_k, cache_meta_v, cache_win_k, cache_win_v, state_hgrn,
              meta_tokens, norm1, w_in, lb_param, hg_norm, attn_sinks, attn_norm, w_out, norm2,
              w_ffn_in, w_ffn_out, final_norm):
    B, S, _ = x_prompt.shape
    Bd, T, _ = x_sample.shape
    nc = S // CHUNK
    keep = min(WINDOW, S)
    lb_all = jnp.cumsum(jax.nn.softmax(lb_param.astype(jnp.float32), axis=0), axis=0)
    pos_meta = jnp.arange(N_META)
    pos_p = N_META + jnp.arange(S)
    pos_s = N_META + PAST_LEN + jnp.arange(T)
    band_valid = (jnp.arange(nc)[:, None] + jnp.arange((WIN_CHUNKS + 1) * CHUNK)[None, :] // CHUNK - WIN_CHUNKS) >= 0
    valid_p = jnp.concatenate([jnp.ones((nc, N_META), dtype=bool), band_valid], axis=-1)

    h = x_prompt
    h_meta = meta_tokens[None].astype(x_prompt.dtype)
    hs = x_sample
    p_mk, p_mv, p_wk, p_wv, p_st, s_nk, s_nv, s_st = ([] for _ in range(8))
    for l in range(DEPTH):
        lb = lb_all[l]
        hm = rmsnorm(h_meta, norm1[l])
        mq, mlogf, mk, mv, mg, maq, mak, mav = project(hm, w_in[l], lb, pos_meta)
        mo_hg, S_meta = hgrn_block(mq, mlogf, mk, mv, jnp.zeros((1, HG_HEADS, HG_DIM, HG_DIM), jnp.float32))
        hn = rmsnorm(h, norm1[l])
        q, logf, k, v, g, aq, ak, av = project(hn, w_in[l], lb, pos_p)
        to_blocks = lambda t: t.reshape(B, nc, CHUNK, *t.shape[2:]).swapaxes(0, 1)

        def step(Sc, blk):
            o, Sn = hgrn_block(*blk, Sc)
            return Sn, o

        S_fin, o_blocks = lax.scan(step, jnp.broadcast_to(S_meta, (B, HG_HEADS, HG_DIM, HG_DIM)),
                                   (to_blocks(q), to_blocks(logf), to_blocks(k), to_blocks(v)))
        o_hg = o_blocks.swapaxes(0, 1).reshape(B, S, HG_HEADS, HG_DIM)
        qb = aq.reshape(B, nc, CHUNK, N_KV_HEADS, Q_PER_KV, ATT_HEAD_DIM)
        meta_kb = jnp.broadcast_to(mak[:, None], (B, nc, N_META, N_KV_HEADS, ATT_HEAD_DIM))
        meta_vb = jnp.broadcast_to(mav[:, None], (B, nc, N_META, N_KV_HEADS, ATT_HEAD_DIM))
        keys = jnp.concatenate([meta_kb, band(ak, nc)], axis=2)
        vals = jnp.concatenate([meta_vb, band(av, nc)], axis=2)
        o_att = sink_attention(qb, keys, vals, attn_sinks[l], valid_p).reshape(B, S, ATT_WIDTH)
        h = finish_layer(h, o_hg, g, o_att, hg_norm[l], attn_norm[l], w_out[l], norm2[l], w_ffn_in[l], w_ffn_out[l])
        p_mk.append(jnp.broadcast_to(mak, (B, N_META, N_KV_HEADS, ATT_HEAD_DIM)))
        p_mv.append(jnp.broadcast_to(mav, (B, N_META, N_KV_HEADS, ATT_HEAD_DIM)))
        p_wk.append(ak[:, S - keep:])
        p_wv.append(av[:, S - keep:])
        p_st.append(S_fin.astype(x_prompt.dtype))
        if l < DEPTH - 1:
            mo_att = sink_attention(maq, mak, mav, attn_sinks[l], None).reshape(1, N_META, ATT_WIDTH)
            h_meta = finish_layer(h_meta, mo_hg, mg, mo_att, hg_norm[l], attn_norm[l], w_out[l], norm2[l], w_ffn_in[l], w_ffn_out[l])
        hn_s = rmsnorm(hs, norm1[l])
        sq, slogf, sk, sv, sg, saq, sak, sav = project(hn_s, w_in[l], lb, pos_s)
        so_hg, S_s = hgrn_block(sq, slogf, sk, sv, state_hgrn[l].astype(jnp.float32))
        skeys = jnp.concatenate([cache_meta_k[l], cache_win_k[l], sak], axis=1)
        svals = jnp.concatenate([cache_meta_v[l], cache_win_v[l], sav], axis=1)
        so_att = sink_attention(saq, skeys, svals, attn_sinks[l], None).reshape(Bd, T, ATT_WIDTH)
        hs = finish_layer(hs, so_hg, sg, so_att, hg_norm[l], attn_norm[l], w_out[l], norm2[l], w_ffn_in[l], w_ffn_out[l])
        s_nk.append(sak)
        s_nv.append(sav)
        s_st.append(S_s.astype(state_hgrn.dtype))

    y_prompt = rmsnorm(h, final_norm)
    y_sample = rmsnorm(hs, final_norm)
    return (y_prompt, y_sample, jnp.stack(p_mk), jnp.stack(p_mv), jnp.stack(p_wk), jnp.stack(p_wv),
            jnp.stack(p_st), jnp.stack(s_nk), jnp.stack(s_nv), jnp.stack(s_st))
```

```python
import functools

import jax
import jax.numpy as jnp
from jax import lax
from jax.experimental import pallas as pl
from jax.experimental.pallas import tpu as pltpu

F32 = jnp.float32
BF16 = jnp.bfloat16

D_MODEL = 1024
CHUNK = 64
N_META = 16
HG_WIDTH = 512
HG_HEADS = 4
HG_DIM = 128
ATT_WIDTH = 512
ATT_HEAD_DIM = 64
N_KV_HEADS = 2
Q_PER_KV = 4
KV_WIDTH = 128
WINDOW = 128
PAST_LEN = 1024
ROPE_THETA = 10000.0
D_FF = 2816
EPS = 1e-6

COL_Q, COL_F, COL_I, COL_G = 0, HG_WIDTH, 2 * HG_WIDTH, 3 * HG_WIDTH
COL_AQ = 4 * HG_WIDTH
COL_AK = COL_AQ + ATT_WIDTH
COL_AV = COL_AK + KV_WIDTH

LANES = 128
SUB_BLOCK = 16
KEY_PAD = 256
NEG_BIG = -1e30
VMEM_LIMIT = 56 * 1024 * 1024

NT_DIMS = (((1,), (1,)), ((), ()))
TN_DIMS = (((0,), (0,)), ((), ()))


def _rms(x, g):
    return x * lax.rsqrt(jnp.mean(x * x, axis=-1, keepdims=True) + EPS) * g


def _sigmoid(x):
    return 1.0 / (1.0 + jnp.exp(-x))


def _silu(x):
    return x * _sigmoid(x)


def _dot(a, b):
    return jnp.dot(a, b, preferred_element_type=F32)


def _forget_lower_bound(lbp_ref):
    p = lbp_ref[...]
    m = jnp.max(p, axis=0, keepdims=True)
    e = jnp.exp(p - m)
    return e[0:1] / jnp.sum(e, axis=0, keepdims=True)


def _cumsum_blocks(x, block):
    t = x.shape[0]
    r = lax.broadcasted_iota(jnp.int32, (t, t), 0)
    c = lax.broadcasted_iota(jnp.int32, (t, t), 1)
    d = r - c
    tri = jnp.where(d >= 0, jnp.where(d <= (r & (block - 1)), 1.0, 0.0), 0.0).astype(BF16)
    a1 = x.astype(BF16)
    r1 = x - a1.astype(F32)
    a2 = r1.astype(BF16)
    a3 = (r1 - a2.astype(F32)).astype(BF16)
    return _dot(tri, a1) + _dot(tri, a2) + _dot(tri, a3)


def _rope(x, tab):
    cos, sin_a, sin_b = tab[:, 0:LANES], tab[:, LANES:2 * LANES], tab[:, 2 * LANES:3 * LANES]
    return x * cos + pltpu.roll(x, 96, 1) * sin_a + pltpu.roll(x, 32, 1) * sin_b


def _dup_heads(x):
    lo = lax.broadcasted_iota(jnp.int32, (1, LANES), 1) < ATT_HEAD_DIM
    sw = pltpu.roll(x, ATT_HEAD_DIM, 1)
    return jnp.where(lo, x, sw), jnp.where(lo, sw, x)


def _project(x, tab, lb, g1, w_in_ref, q_ref, k_ref, v_ref, b_ref, g_ref, aq_ref, block):
    hn = _rms(x, g1).astype(BF16)
    col = lambda c0, w: _dot(hn, w_in_ref[:, c0:c0 + w])
    q_ref[...] = _silu(col(COL_Q, HG_WIDTH))
    f = lb + (1.0 - lb) * _sigmoid(col(COL_F, HG_WIDTH))
    k_ref[...] = 1.0 - f
    b_ref[...] = _cumsum_blocks(jnp.log(f), block)
    v_ref[...] = col(COL_I, HG_WIDTH)
    g_ref[...] = col(COL_G, HG_WIDTH)
    for j in range(ATT_WIDTH // LANES):
        aq = _rope(col(COL_AQ + j * LANES, LANES), tab)
        aq_ref[:, j * LANES:(j + 1) * LANES] = aq * (ATT_HEAD_DIM ** -0.5)
    ak = _rope(col(COL_AK, KV_WIDTH), tab)
    av = col(COL_AV, KV_WIDTH)
    return ak, av


def _hgrn_chunk(q_ref, k_ref, v_ref, b_ref, st_ref, o_ref, r0, length):
    row8 = lax.broadcasted_iota(jnp.int32, (8, 1), 0)
    rowl = lax.broadcasted_iota(jnp.int32, (length, 1), 0)
    nb = length // SUB_BLOCK
    for h in range(HG_HEADS):
        ls = slice(h * HG_DIM, (h + 1) * HG_DIM)
        rows = lambda ref, a, n: ref.at[pl.ds(r0, length)][a:a + n, ls]
        q, k, v, b = (rows(r, 0, length) for r in (q_ref, k_ref, v_ref, b_ref))
        b_last = rows(b_ref, length - 1, 1)
        st = st_ref[h]
        v16 = v.astype(BF16)
        o_inter = lax.dot_general((q * jnp.exp(b)).astype(BF16), st.astype(BF16), NT_DIMS,
                                  preferred_element_type=F32)
        k_end = (k * jnp.exp(b_last - b)).astype(BF16)
        st_ref[h] = st * jnp.exp(b_last) + lax.dot_general(v16, k_end, TN_DIMS, preferred_element_type=F32)
        a_off = []
        for i in range(1, nb):
            c_i = rows(b_ref, SUB_BLOCK * i - 1, 1)
            rs = slice(SUB_BLOCK * i, SUB_BLOCK * (i + 1))
            q_i = (q[rs] * jnp.exp(b[rs] - c_i)).astype(BF16)
            k_i = jnp.where(rowl < SUB_BLOCK * i, k * jnp.exp(jnp.minimum(c_i - b, 0.0)), 0.0).astype(BF16)
            a_off.append(lax.dot_general(q_i, k_i, NT_DIMS, preferred_element_type=F32))
        if a_off:
            o_off = _dot(jnp.concatenate(a_off, axis=0).astype(BF16), v16)
        out_rows = []
        for g8 in range(length // 8):
            t0 = 8 * g8
            blk0 = (t0 // SUB_BLOCK) * SUB_BLOCK
            q8, b8 = q[t0:t0 + 8], b[t0:t0 + 8]
            acc = o_inter[t0:t0 + 8]
            if blk0 > 0:
                acc = acc + o_off[t0 - SUB_BLOCK:t0 - SUB_BLOCK + 8]
            for s in range(blk0, t0 + 8):
                b_s, k_s, v_s = rows(b_ref, s, 1), rows(k_ref, s, 1), rows(v_ref, s, 1)
                w = jnp.sum(q8 * k_s * jnp.exp(b8 - b_s), axis=-1, keepdims=True)
                if s >= t0:
                    w = jnp.where(row8 >= s - t0, w, 0.0)
                acc = acc + w * v_s
            out_rows.append(acc)
        o_ref[pl.ds(r0, length), ls] = jnp.concatenate(out_rows, axis=0)


def _attn_chunk(aq_ref, ks_ref, vs_ref, mk_ref, mv_ref, sinks_ref, o_ref, r0, nq, kb0, nband, first_valid):
    lo = lax.broadcasted_iota(jnp.int32, (1, LANES), 1) < ATT_HEAD_DIM
    kidx = lax.broadcasted_iota(jnp.int32, (1, KEY_PAD), 1)
    valid = jnp.logical_and(kidx >= first_valid, kidx < nband + N_META)
    rid = lax.broadcasted_iota(jnp.int32, (Q_PER_KV * nq, 1), 0)
    pad = jnp.zeros((KEY_PAD - nband - N_META, LANES), BF16)
    for g in range(N_KV_HEADS):
        keys = jnp.concatenate([ks_ref[g, pl.ds(kb0, nband), :], mk_ref[g], pad], axis=0)
        vals = jnp.concatenate([vs_ref[g, pl.ds(kb0, nband), :], mv_ref[g], pad], axis=0)
        qs = []
        for j in range(2):
            c0 = (2 * g + j) * LANES
            qj = aq_ref[pl.ds(r0, nq), c0:c0 + LANES]
            qs += [jnp.where(lo, qj, 0.0), jnp.where(lo, 0.0, qj)]
        qst = jnp.concatenate(qs, axis=0).astype(BF16)
        s = lax.dot_general(qst, keys, NT_DIMS, preferred_element_type=F32)
        s = jnp.where(valid, s, NEG_BIG)
        sink = jnp.where(rid < nq, sinks_ref[g, 0],
                         jnp.where(rid < 2 * nq, sinks_ref[g, 1],
                                   jnp.where(rid < 3 * nq, sinks_ref[g, 2], sinks_ref[g, 3])))
        m = jnp.maximum(jnp.max(s, axis=-1, keepdims=True), sink)
        p = jnp.exp(s - m)
        den = jnp.sum(p, axis=-1, keepdims=True) + jnp.exp(sink - m)
        pv = _dot(p.astype(BF16), vals) / den
        for j in range(2):
            c0 = (2 * g + j) * LANES
            o_ref[pl.ds(r0, nq), c0:c0 + LANES] = jnp.where(
                lo, pv[(2 * j) * nq:(2 * j + 1) * nq], pv[(2 * j + 1) * nq:(2 * j + 2) * nq])


def _finish(x, ohg_ref, g_ref, oatt_ref, hgg, ang, n2, fn, w_out_ref, w_fi_ref, w_fo_ref):
    parts = []
    for h in range(HG_HEADS):
        ls = slice(h * HG_DIM, (h + 1) * HG_DIM)
        o = ohg_ref[:, ls]
        on = o * lax.rsqrt(jnp.mean(o * o, axis=-1, keepdims=True) + EPS)
        parts.append(on * hgg[:, ls] * _silu(g_ref[:, ls]))
    parts.append(_rms(oatt_ref[...], ang))
    mixed = _dot(jnp.concatenate(parts, axis=-1).astype(BF16), w_out_ref[...])
    h1 = x + mixed
    hn2 = _rms(h1, n2).astype(BF16)
    half = D_FF // 2
    acc = h1
    for c in range(2):
        gate = _dot(hn2, w_fi_ref[:, c * half:(c + 1) * half])
        up = _dot(hn2, w_fi_ref[:, D_FF + c * half:D_FF + (c + 1) * half])
        acc = acc + _dot((_silu(gate) * up).astype(BF16), w_fo_ref[c * half:(c + 1) * half, :])
    return _rms(acc, fn)


def _store_window(ak, av, ks_ref, vs_ref, t):
    k0, k1 = _dup_heads(ak)
    v0, v1 = _dup_heads(av)
    ks_ref[0, WINDOW:WINDOW + t, :] = k0.astype(BF16)
    ks_ref[1, WINDOW:WINDOW + t, :] = k1.astype(BF16)
    vs_ref[0, WINDOW:WINDOW + t, :] = v0.astype(BF16)
    vs_ref[1, WINDOW:WINDOW + t, :] = v1.astype(BF16)


def _meta_kernel(x_ref, tab_ref, lbp_ref, g1_ref, w_in_ref,
                 mak_ref, mav_ref, mk2_ref, mv2_ref, st_ref):
    hn = _rms(x_ref[...], g1_ref[...]).astype(BF16)
    col = lambda c0, w: _dot(hn, w_in_ref[:, c0:c0 + w])
    lb = _forget_lower_bound(lbp_ref)
    f = lb + (1.0 - lb) * _sigmoid(col(COL_F, HG_WIDTH))
    b = _cumsum_blocks(jnp.log(f), N_META)
    k_end = ((1.0 - f) * jnp.exp(b[N_META - 1:N_META] - b)).astype(BF16)
    v16 = col(COL_I, HG_WIDTH).astype(BF16)
    for h in range(HG_HEADS):
        ls = slice(h * HG_DIM, (h + 1) * HG_DIM)
        st_ref[h] = lax.dot_general(v16[:, ls], k_end[:, ls], TN_DIMS, preferred_element_type=F32)
    ak = _rope(col(COL_AK, KV_WIDTH), tab_ref[...])
    av = col(COL_AV, KV_WIDTH)
    mak_ref[...] = ak
    mav_ref[...] = av
    k0, k1 = _dup_heads(ak)
    v0, v1 = _dup_heads(av)
    mk2_ref[0], mk2_ref[1] = k0.astype(BF16), k1.astype(BF16)
    mv2_ref[0], mv2_ref[1] = v0.astype(BF16), v1.astype(BF16)


def _prompt_kernel(x_ref, tab_ref, stm_ref, mk2_ref, mv2_ref, sinks_ref, lbp_ref, g1_ref, hgg_ref, ang_ref,
                   n2_ref, fn_ref, w_in_ref, w_out_ref, w_fi_ref, w_fo_ref,
                   y_ref, wk_ref, wv_ref, sto_ref,
                   q_s, k_s, v_s, b_s, g_s, aq_s, ohg_s, oatt_s, st_s, ks_s, vs_s, *, tile):
    j = pl.program_id(1)
    n_chunks = tile // CHUNK

    @pl.when(j == 0)
    def _():
        st_s[...] = stm_ref[...]
        ks_s[:, 0:WINDOW, :] = jnp.zeros((N_KV_HEADS, WINDOW, LANES), BF16)
        vs_s[:, 0:WINDOW, :] = jnp.zeros((N_KV_HEADS, WINDOW, LANES), BF16)

    x = x_ref[0]
    lb = _forget_lower_bound(lbp_ref)
    ak, av = _project(x, tab_ref[...], lb, g1_ref[...], w_in_ref, q_s, k_s, v_s, b_s, g_s, aq_s, CHUNK)
    _store_window(ak, av, ks_s, vs_s, tile)

    def chunk_body(c, carry):
        r0 = pl.multiple_of(c * CHUNK, CHUNK)
        _hgrn_chunk(q_s, k_s, v_s, b_s, st_s, ohg_s, r0, CHUNK)
        first_valid = jnp.maximum(WINDOW - (j * n_chunks + c) * CHUNK, 0)
        _attn_chunk(aq_s, ks_s, vs_s, mk2_ref, mv2_ref, sinks_ref, oatt_s, r0, CHUNK, r0,
                    WINDOW + CHUNK, first_valid)
        return carry

    lax.fori_loop(0, n_chunks, chunk_body, 0)

    y_ref[0] = _finish(x, ohg_s, g_s, oatt_s, hgg_ref[...], ang_ref[...], n2_ref[...], fn_ref[...],
                       w_out_ref, w_fi_ref, w_fo_ref)
    ks_s[:, 0:WINDOW, :] = ks_s[:, tile:tile + WINDOW, :]
    vs_s[:, 0:WINDOW, :] = vs_s[:, tile:tile + WINDOW, :]

    @pl.when(j == pl.num_programs(1) - 1)
    def _():
        wk_ref[0] = ak[tile - WINDOW:tile]
        wv_ref[0] = av[tile - WINDOW:tile]
        for h in range(HG_HEADS):
            sto_ref[0, h] = st_s[h].T


def _sample_kernel(x_ref, tab_ref, st0_ref, cmk_ref, cmv_ref, cwk_ref, cwv_ref, sinks_ref, lbp_ref, g1_ref,
                   hgg_ref, ang_ref, n2_ref, fn_ref, w_in_ref, w_out_ref, w_fi_ref, w_fo_ref,
                   y_ref, nk_ref, nv_ref, sto_ref,
                   q_s, k_s, v_s, b_s, g_s, aq_s, ohg_s, oatt_s, st_s, ks_s, vs_s, mk_s, mv_s, *, tile):
    for h in range(HG_HEADS):
        st_s[h] = st0_ref[0, h].T
    for src, dst in ((cmk_ref, mk_s), (cmv_ref, mv_s)):
        d0, d1 = _dup_heads(src[0])
        dst[0], dst[1] = d0.astype(BF16), d1.astype(BF16)
    for src, dst in ((cwk_ref, ks_s), (cwv_ref, vs_s)):
        d0, d1 = _dup_heads(src[0])
        dst[0, 0:WINDOW, :], dst[1, 0:WINDOW, :] = d0.astype(BF16), d1.astype(BF16)

    x = x_ref[0]
    lb = _forget_lower_bound(lbp_ref)
    ak, av = _project(x, tab_ref[...], lb, g1_ref[...], w_in_ref, q_s, k_s, v_s, b_s, g_s, aq_s, tile)
    _store_window(ak, av, ks_s, vs_s, tile)
    _hgrn_chunk(q_s, k_s, v_s, b_s, st_s, ohg_s, 0, tile)
    _attn_chunk(aq_s, ks_s, vs_s, mk_s, mv_s, sinks_ref, oatt_s, 0, tile, 0, WINDOW + tile, 0)
    y_ref[0] = _finish(x, ohg_s, g_s, oatt_s, hgg_ref[...], ang_ref[...], n2_ref[...], fn_ref[...],
                       w_out_ref, w_fi_ref, w_fo_ref)
    nk_ref[0] = ak
    nv_ref[0] = av
    for h in range(HG_HEADS):
        sto_ref[0, h] = st_s[h].T


def _rope_table(pos):
    inv = ROPE_THETA ** (-jnp.arange(0, ATT_HEAD_DIM, 2, dtype=F32) / ATT_HEAD_DIM)
    ang = pos.astype(F32)[:, None] * inv[None, :]
    ang = jnp.concatenate([ang, ang, ang, ang], axis=-1)
    first = (jnp.arange(LANES) % ATT_HEAD_DIM) < ATT_HEAD_DIM // 2
    cos, sin = jnp.cos(ang), jnp.sin(ang)
    return jnp.concatenate([cos, jnp.where(first, -sin, 0.0), jnp.where(first, 0.0, sin)], axis=-1)


def _vmem():
    return pl.BlockSpec(memory_space=pltpu.VMEM)


def _tile_scratch(tile):
    wide = lambda: pltpu.VMEM((tile, HG_WIDTH), F32)
    return [wide() for _ in range(8)] + [
        pltpu.VMEM((HG_HEADS, HG_DIM, HG_DIM), F32),
        pltpu.VMEM((N_KV_HEADS, WINDOW + tile, LANES), BF16),
        pltpu.VMEM((N_KV_HEADS, WINDOW + tile, LANES), BF16),
    ]


def _prompt_tile(seq):
    for t in (256, 128):
        if seq % t == 0:
            return t
    raise ValueError(f"sequence length {seq} must be a multiple of 128")


def kernel(x_prompt, x_sample, cache_meta_k, cache_meta_v, cache_win_k, cache_win_v, state_hgrn, meta_tokens,
           norm1, w_in, lb_param, hg_norm, attn_sinks, attn_norm, w_out, norm2, w_ffn_in, w_ffn_out, final_norm):
    B, S, D = x_prompt.shape
    Bd, T, _ = x_sample.shape
    assert norm1.shape[0] == 1 and lb_param.shape[0] == 2, "one-layer model"
    assert D == D_MODEL and T % SUB_BLOCK == 0 and T <= CHUNK and cache_win_k.shape[2] == WINDOW
    tile = _prompt_tile(S)

    w_in_b, w_out_b = w_in[0].astype(BF16), w_out[0].astype(BF16)
    w_fi_b, w_fo_b = w_ffn_in[0].astype(BF16), w_ffn_out[0].astype(BF16)
    row = lambda a: a.reshape(1, -1).astype(F32)
    g1, hgg, ang, n2, fn = row(norm1[0]), row(hg_norm[0]), row(attn_norm[0]), row(norm2[0]), row(final_norm)
    lbp = lb_param.astype(F32)
    sinks = attn_sinks[0].astype(F32)
    tab_m = _rope_table(jnp.arange(N_META))
    tab_p = _rope_table(N_META + jnp.arange(S))
    tab_s = _rope_table(N_META + PAST_LEN + jnp.arange(T))
    params = pltpu.CompilerParams(vmem_limit_bytes=VMEM_LIMIT)

    mak, mav, mk2, mv2, stm = pl.pallas_call(
        _meta_kernel,
        out_shape=(jax.ShapeDtypeStruct((N_META, KV_WIDTH), F32), jax.ShapeDtypeStruct((N_META, KV_WIDTH), F32),
                   jax.ShapeDtypeStruct((N_KV_HEADS, N_META, LANES), BF16),
                   jax.ShapeDtypeStruct((N_KV_HEADS, N_META, LANES), BF16),
                   jax.ShapeDtypeStruct((HG_HEADS, HG_DIM, HG_DIM), F32)),
        in_specs=[_vmem()] * 5, out_specs=tuple(_vmem() for _ in range(5)),
        compiler_params=params, name="meta",
    )(meta_tokens.astype(F32), tab_m, lbp, g1, w_in_b)

    smem = pl.BlockSpec(memory_space=pltpu.SMEM)
    weights_specs = [_vmem()] * 4
    y_p, wk, wv, st_p = pl.pallas_call(
        functools.partial(_prompt_kernel, tile=tile),
        grid=(B, S // tile),
        out_shape=(jax.ShapeDtypeStruct((B, S, D), F32), jax.ShapeDtypeStruct((B, WINDOW, KV_WIDTH), F32),
                   jax.ShapeDtypeStruct((B, WINDOW, KV_WIDTH), F32),
                   jax.ShapeDtypeStruct((B, HG_HEADS, HG_DIM, HG_DIM), F32)),
        in_specs=[pl.BlockSpec((1, tile, D), lambda b, j: (b, j, 0)),
                  pl.BlockSpec((tile, 3 * LANES), lambda b, j: (j, 0)),
                  _vmem(), _vmem(), _vmem(), smem] + [_vmem()] * 6 + weights_specs,
        out_specs=(pl.BlockSpec((1, tile, D), lambda b, j: (b, j, 0)),
                   pl.BlockSpec((1, WINDOW, KV_WIDTH), lambda b, j: (b, 0, 0)),
                   pl.BlockSpec((1, WINDOW, KV_WIDTH), lambda b, j: (b, 0, 0)),
                   pl.BlockSpec((1, HG_HEADS, HG_DIM, HG_DIM), lambda b, j: (b, 0, 0, 0))),
        scratch_shapes=_tile_scratch(tile),
        compiler_params=pltpu.CompilerParams(vmem_limit_bytes=VMEM_LIMIT,
                                             dimension_semantics=("arbitrary", "arbitrary")),
        name="prompt",
    )(x_prompt, tab_p, stm, mk2, mv2, sinks, lbp, g1, hgg, ang, n2, fn, w_in_b, w_out_b, w_fi_b, w_fo_b)

    per_stream3 = lambda n, w: pl.BlockSpec((1, n, w), lambda b: (b, 0, 0))
    state_spec = pl.BlockSpec((1, HG_HEADS, HG_DIM, HG_DIM), lambda b: (b, 0, 0, 0))
    y_s, nk, nv, st_s = pl.pallas_call(
        functools.partial(_sample_kernel, tile=T),
        grid=(Bd,),
        out_shape=(jax.ShapeDtypeStruct((Bd, T, D), F32), jax.ShapeDtypeStruct((Bd, T, KV_WIDTH), F32),
                   jax.ShapeDtypeStruct((Bd, T, KV_WIDTH), F32),
                   jax.ShapeDtypeStruct((Bd, HG_HEADS, HG_DIM, HG_DIM), F32)),
        in_specs=[per_stream3(T, D), _vmem(), state_spec,
                  per_stream3(N_META, KV_WIDTH), per_stream3(N_META, KV_WIDTH),
                  per_stream3(WINDOW, KV_WIDTH), per_stream3(WINDOW, KV_WIDTH),
                  smem] + [_vmem()] * 6 + weights_specs,
        out_specs=(per_stream3(T, D), per_stream3(T, KV_WIDTH), per_stream3(T, KV_WIDTH), state_spec),
        scratch_shapes=_tile_scratch(T) + [pltpu.VMEM((N_KV_HEADS, N_META, LANES), BF16),
                                           pltpu.VMEM((N_KV_HEADS, N_META, LANES), BF16)],
        compiler_params=pltpu.CompilerParams(vmem_limit_bytes=VMEM_LIMIT, dimension_semantics=("arbitrary",)),
        name="sample",
    )(x_sample, tab_s, state_hgrn[0].astype(F32),
      cache_meta_k[0].reshape(Bd, N_META, KV_WIDTH), cache_meta_v[0].reshape(Bd, N_META, KV_WIDTH),
      cache_win_k[0].reshape(Bd, WINDOW, KV_WIDTH), cache_win_v[0].reshape(Bd, WINDOW, KV_WIDTH),
      sinks, lbp, g1, hgg, ang, n2, fn, w_in_b, w_out_b, w_fi_b, w_fo_b)

    kv5 = lambda a, n, t: a.reshape(1, n, t, N_KV_HEADS, ATT_HEAD_DIM)
    bmeta = lambda a: jnp.broadcast_to(a.reshape(1, 1, N_META, N_KV_HEADS, ATT_HEAD_DIM),
                                       (1, B, N_META, N_KV_HEADS, ATT_HEAD_DIM))
    return (y_p, y_s, bmeta(mak), bmeta(mav), kv5(wk, B, WINDOW), kv5(wv, B, WINDOW), st_p[None],
            kv5(nk, Bd, T), kv5(nv, Bd, T), st_s[None])
```

```python
import functools

import jax
import jax.numpy as jnp
from jax import lax
from jax.experimental import pallas as pl
from jax.experimental.pallas import tpu as pltpu

F32 = jnp.float32
BF16 = jnp.bfloat16

D_MODEL = 1024
CHUNK = 64
N_META = 16
HG_WIDTH = 512
HG_HEADS = 4
HG_DIM = 128
ATT_WIDTH = 512
ATT_HEAD_DIM = 64
N_KV_HEADS = 2
Q_PER_KV = 4
KV_WIDTH = 128
WINDOW = 128
PAST_LEN = 1024
ROPE_THETA = 10000.0
D_FF = 2816
EPS = 1e-6

COL_Q, COL_F, COL_I, COL_G = 0, HG_WIDTH, 2 * HG_WIDTH, 3 * HG_WIDTH
COL_AQ = 4 * HG_WIDTH
COL_AK = COL_AQ + ATT_WIDTH
COL_AV = COL_AK + KV_WIDTH

LANES = 128
SUB_BLOCK = 16
MXU_TILE = 256
KEY_PAD = MXU_TILE
NEG_BIG = -1e30
VMEM_LIMIT = 56 * 1024 * 1024
FFN_PARTS = 6

NT_DIMS = (((1,), (1,)), ((), ()))
TN_DIMS = (((0,), (0,)), ((), ()))


def _rms(x, g):
    return x * lax.rsqrt(jnp.mean(x * x, axis=-1, keepdims=True) + EPS) * g


def _sigmoid(x):
    return 1.0 / (1.0 + jnp.exp(-x))


def _silu(x):
    return x * _sigmoid(x)


def _dot(a, b):
    return jnp.dot(a, b, preferred_element_type=F32)


def _forget_lower_bound(lbp_ref):
    p = lbp_ref[...]
    m = jnp.max(p, axis=0, keepdims=True)
    e = jnp.exp(p - m)
    return e[0:1] / jnp.sum(e, axis=0, keepdims=True)


def _cumsum_blocks(x, block):
    t = x.shape[0]
    r = lax.broadcasted_iota(jnp.int32, (t, t), 0)
    c = lax.broadcasted_iota(jnp.int32, (t, t), 1)
    d = r - c
    tri = jnp.where(d >= 0, jnp.where(d <= (r & (block - 1)), 1.0, 0.0), 0.0).astype(BF16)
    a1 = x.astype(BF16)
    r1 = x - a1.astype(F32)
    a2 = r1.astype(BF16)
    a3 = (r1 - a2.astype(F32)).astype(BF16)
    return _dot(tri, a1) + _dot(tri, a2) + _dot(tri, a3)


def _rope(x, tab):
    cos, sin_a, sin_b = tab[:, 0:LANES], tab[:, LANES:2 * LANES], tab[:, 2 * LANES:3 * LANES]
    return x * cos + pltpu.roll(x, 96, 1) * sin_a + pltpu.roll(x, 32, 1) * sin_b


def _dup_heads(x):
    lo = lax.broadcasted_iota(jnp.int32, (1, LANES), 1) < ATT_HEAD_DIM
    sw = pltpu.roll(x, ATT_HEAD_DIM, 1)
    return jnp.where(lo, x, sw), jnp.where(lo, sw, x)


def _project(x, tab, lb, g1, w_in_ref, q_ref, k_ref, v_ref, b_ref, g_ref, aq_ref, block):
    hn = _rms(x, g1).astype(BF16)
    col = lambda c0, w: _dot(hn, w_in_ref[:, c0:c0 + w])
    g_ref[...] = col(COL_G, HG_WIDTH)
    q_ref[...] = _silu(col(COL_Q, HG_WIDTH))
    f = lb + (1.0 - lb) * _sigmoid(col(COL_F, HG_WIDTH))
    k_ref[...] = 1.0 - f
    b_ref[...] = _cumsum_blocks(jnp.log(f), block)
    v_ref[...] = col(COL_I, HG_WIDTH)
    aq = col(COL_AQ, ATT_WIDTH)
    for j in range(ATT_WIDTH // LANES):
        ls = slice(j * LANES, (j + 1) * LANES)
        aq_ref[:, ls] = _rope(aq[:, ls], tab) * (ATT_HEAD_DIM ** -0.5)
    akv = col(COL_AK, 2 * KV_WIDTH)
    return _rope(akv[:, 0:KV_WIDTH], tab), akv[:, KV_WIDTH:2 * KV_WIDTH]


def _hgrn_steps(q_ref, k_ref, v_ref, b_ref, st_ref, o_ref, r0, length):
    row8 = lax.broadcasted_iota(jnp.int32, (8, 1), 0)
    rowl = lax.broadcasted_iota(jnp.int32, (length, 1), 0)
    nb = length // SUB_BLOCK
    for h in range(HG_HEADS):
        ls = slice(h * HG_DIM, (h + 1) * HG_DIM)
        rows = lambda ref, a, n: ref[r0 + a:r0 + a + n, ls]
        q, k, v, b = (rows(r, 0, length) for r in (q_ref, k_ref, v_ref, b_ref))
        b_last = rows(b_ref, length - 1, 1)
        st = st_ref[h]
        v16 = v.astype(BF16)
        o_part = lax.dot_general((q * jnp.exp(b)).astype(BF16), st.astype(BF16), NT_DIMS,
                                 preferred_element_type=F32)
        k_end = (k * jnp.exp(b_last - b)).astype(BF16)
        st_ref[h] = st * jnp.exp(b_last) + lax.dot_general(v16, k_end, TN_DIMS, preferred_element_type=F32)
        a_off = [jnp.zeros((SUB_BLOCK, length), F32)]
        for i in range(1, nb):
            c_i = rows(b_ref, SUB_BLOCK * i - 1, 1)
            rs = slice(SUB_BLOCK * i, SUB_BLOCK * (i + 1))
            q_i = (q[rs] * jnp.exp(b[rs] - c_i)).astype(BF16)
            k_i = jnp.where(rowl < SUB_BLOCK * i, k * jnp.exp(jnp.minimum(c_i - b, 0.0)), 0.0).astype(BF16)
            a_off.append(lax.dot_general(q_i, k_i, NT_DIMS, preferred_element_type=F32))
        if nb > 1:
            o_part = o_part + _dot(jnp.concatenate(a_off, axis=0).astype(BF16), v16)
        o_ref[r0:r0 + length, ls] = o_part
        yield
        for g8 in range(length // 8):
            t0 = 8 * g8
            blk0 = (t0 // SUB_BLOCK) * SUB_BLOCK
            q8, b8 = rows(q_ref, t0, 8), rows(b_ref, t0, 8)
            acc = rows(o_ref, t0, 8)
            for s in range(blk0, t0 + 8):
                b_s, k_s, v_s = rows(b_ref, s, 1), rows(k_ref, s, 1), rows(v_ref, s, 1)
                w = jnp.sum(q8 * k_s * jnp.exp(b8 - b_s), axis=-1, keepdims=True)
                if s >= t0:
                    w = jnp.where(row8 >= s - t0, w, 0.0)
                acc = acc + w * v_s
            o_ref[r0 + t0:r0 + t0 + 8, ls] = acc
            yield


def _attn_steps(aq_ref, ks_ref, vs_ref, mk_ref, mv_ref, sinks_ref, o_ref, r0, nq, kb0, nband, first_valid):
    lo = lax.broadcasted_iota(jnp.int32, (1, LANES), 1) < ATT_HEAD_DIM
    kidx = lax.broadcasted_iota(jnp.int32, (1, KEY_PAD), 1)
    valid = jnp.logical_and(kidx >= first_valid, kidx < nband + N_META)
    rid = lax.broadcasted_iota(jnp.int32, (Q_PER_KV * nq, 1), 0)
    pad = jnp.zeros((KEY_PAD - nband - N_META, LANES), BF16)
    for g in range(N_KV_HEADS):
        keys = jnp.concatenate([ks_ref[g, kb0:kb0 + nband, :], mk_ref[g], pad], axis=0)
        vals = jnp.concatenate([vs_ref[g, kb0:kb0 + nband, :], mv_ref[g], pad], axis=0)
        qs = []
        for j in range(2):
            c0 = (2 * g + j) * LANES
            qj = aq_ref[r0:r0 + nq, c0:c0 + LANES]
            qs += [jnp.where(lo, qj, 0.0), jnp.where(lo, 0.0, qj)]
        qst = jnp.concatenate(qs, axis=0).astype(BF16)
        s = lax.dot_general(qst, keys, NT_DIMS, preferred_element_type=F32)
        s = jnp.where(valid, s, NEG_BIG)
        sink = jnp.where(rid < nq, sinks_ref[g, 0],
                         jnp.where(rid < 2 * nq, sinks_ref[g, 1],
                                   jnp.where(rid < 3 * nq, sinks_ref[g, 2], sinks_ref[g, 3])))
        m = jnp.maximum(jnp.max(s, axis=-1, keepdims=True), sink)
        p = jnp.exp(s - m)
        den = jnp.sum(p, axis=-1, keepdims=True) + jnp.exp(sink - m)
        pv = _dot(p.astype(BF16), vals) / den
        for j in range(2):
            c0 = (2 * g + j) * LANES
            o_ref[r0:r0 + nq, c0:c0 + LANES] = jnp.where(
                lo, pv[(2 * j) * nq:(2 * j + 1) * nq], pv[(2 * j + 1) * nq:(2 * j + 2) * nq])
        yield


def _ffn_steps(h1, hn2, w_fi_ref, w_fo_ref, parts, out):
    acc = h1
    for c0, c1 in _ffn_bounds(parts):
        gate = _dot(hn2, w_fi_ref[:, c0:c1])
        yield
        up = _dot(hn2, w_fi_ref[:, D_FF + c0:D_FF + c1])
        yield
        acc = acc + _dot((_silu(gate) * up).astype(BF16), w_fo_ref[c0:c1, :])
        yield
    out.append(acc)


def _run(*gens):
    for g in gens:
        for _ in g:
            pass


def _interleave(main, n_main, filler, n_filler):
    done = 0
    for i in range(n_main):
        next(main, None)
        target = (n_filler * (i + 1) + n_main - 1) // n_main
        while done < target:
            next(filler, None)
            done += 1
    _run(main, filler)


def _mix_out(x, ohg_ref, g_ref, oatt_ref, hgg, ang, w_out_ref):
    parts = []
    for h in range(HG_HEADS):
        ls = slice(h * HG_DIM, (h + 1) * HG_DIM)
        o = ohg_ref[:, ls]
        on = o * lax.rsqrt(jnp.mean(o * o, axis=-1, keepdims=True) + EPS)
        parts.append(on * hgg[:, ls] * _silu(g_ref[:, ls]))
    parts.append(_rms(oatt_ref[...], ang))
    mixed = _dot(jnp.concatenate(parts, axis=-1).astype(BF16), w_out_ref[...])
    return x + mixed


def _ffn_bounds(parts):
    tiles = D_FF // MXU_TILE
    cuts = [MXU_TILE * ((tiles * p + parts - 1) // parts) for p in range(parts + 1)]
    return list(zip(cuts[:-1], cuts[1:]))


def _store_window(ak, av, ks_ref, vs_ref, t):
    k0, k1 = _dup_heads(ak)
    v0, v1 = _dup_heads(av)
    ks_ref[0, WINDOW:WINDOW + t, :] = k0.astype(BF16)
    ks_ref[1, WINDOW:WINDOW + t, :] = k1.astype(BF16)
    vs_ref[0, WINDOW:WINDOW + t, :] = v0.astype(BF16)
    vs_ref[1, WINDOW:WINDOW + t, :] = v1.astype(BF16)


def _meta_kernel(x_ref, tab_ref, lbp_ref, g1_ref, w_in_ref,
                 mak_ref, mav_ref, mk2_ref, mv2_ref, st_ref):
    hn = _rms(x_ref[...], g1_ref[...]).astype(BF16)
    col = lambda c0, w: _dot(hn, w_in_ref[:, c0:c0 + w])
    lb = _forget_lower_bound(lbp_ref)
    f = lb + (1.0 - lb) * _sigmoid(col(COL_F, HG_WIDTH))
    b = _cumsum_blocks(jnp.log(f), N_META)
    k_end = ((1.0 - f) * jnp.exp(b[N_META - 1:N_META] - b)).astype(BF16)
    v16 = col(COL_I, HG_WIDTH).astype(BF16)
    for h in range(HG_HEADS):
        ls = slice(h * HG_DIM, (h + 1) * HG_DIM)
        st_ref[h] = lax.dot_general(v16[:, ls], k_end[:, ls], TN_DIMS, preferred_element_type=F32)
    ak = _rope(col(COL_AK, KV_WIDTH), tab_ref[...])
    av = col(COL_AV, KV_WIDTH)
    mak_ref[...] = ak
    mav_ref[...] = av
    k0, k1 = _dup_heads(ak)
    v0, v1 = _dup_heads(av)
    mk2_ref[0], mk2_ref[1] = k0.astype(BF16), k1.astype(BF16)
    mv2_ref[0], mv2_ref[1] = v0.astype(BF16), v1.astype(BF16)


def _prompt_kernel(x_ref, tab_ref, stm_ref, mk2_ref, mv2_ref, sinks_ref, lbp_ref, g1_ref, hgg_ref, ang_ref,
                   n2_ref, fn_ref, w_in_ref, w_out_ref, w_fi_ref, w_fo_ref,
                   y_ref, wk_ref, wv_ref, sto_ref,
                   q_s, k_s, v_s, b_s, aq_s, st_s, ks_s, vs_s, x2_s, g2_s, ohg2_s, oatt2_s,
                   *, tile, tiles_per_stream, n_tiles):
    s = pl.program_id(0)
    cur = lax.rem(s, 2)
    prev = 1 - cur
    j = lax.rem(jnp.minimum(s, n_tiles - 1), tiles_per_stream)

    @pl.when(s == 0)
    def _():
        for ref in (g2_s, ohg2_s, oatt2_s):
            ref[1] = jnp.zeros((tile, HG_WIDTH), F32)
        x2_s[1] = jnp.zeros((tile, D_MODEL), F32)

    @pl.when(j == 0)
    def _():
        st_s[...] = stm_ref[...]
        ks_s[:, 0:WINDOW, :] = jnp.zeros((N_KV_HEADS, WINDOW, LANES), BF16)
        vs_s[:, 0:WINDOW, :] = jnp.zeros((N_KV_HEADS, WINDOW, LANES), BF16)

    x = x_ref[0]
    x2_s[cur] = x
    lb = _forget_lower_bound(lbp_ref)
    ak, av = _project(x, tab_ref[...], lb, g1_ref[...], w_in_ref, q_s, k_s, v_s, b_s, g2_s.at[cur], aq_s, CHUNK)
    _store_window(ak, av, ks_s, vs_s, tile)

    n_chunks = tile // CHUNK
    h1 = _mix_out(x2_s[prev], ohg2_s.at[prev], g2_s.at[prev], oatt2_s.at[prev], hgg_ref[...], ang_ref[...],
                  w_out_ref)
    hn2 = _rms(h1, n2_ref[...]).astype(BF16)

    def mixer_steps():
        for c in range(n_chunks):
            r0 = c * CHUNK
            yield from _hgrn_steps(q_s, k_s, v_s, b_s, st_s, ohg2_s.at[cur], r0, CHUNK)
            first_valid = jnp.maximum(WINDOW - (j * n_chunks + c) * CHUNK, 0)
            yield from _attn_steps(aq_s, ks_s, vs_s, mk2_ref, mv2_ref, sinks_ref, oatt2_s.at[cur], r0, CHUNK, r0,
                                   WINDOW + CHUNK, first_valid)

    n_mixer = n_chunks * (HG_HEADS * (1 + CHUNK // 8) + N_KV_HEADS)
    ffn_out = []
    _interleave(_ffn_steps(h1, hn2, w_fi_ref, w_fo_ref, FFN_PARTS, ffn_out), 3 * FFN_PARTS,
                mixer_steps(), n_mixer)
    y_ref[0] = _rms(ffn_out[0], fn_ref[...])
    ks_s[:, 0:WINDOW, :] = ks_s[:, tile:tile + WINDOW, :]
    vs_s[:, 0:WINDOW, :] = vs_s[:, tile:tile + WINDOW, :]

    @pl.when(jnp.logical_and(j == tiles_per_stream - 1, s < n_tiles))
    def _():
        wk_ref[0] = ak[tile - WINDOW:tile]
        wv_ref[0] = av[tile - WINDOW:tile]
        for h in range(HG_HEADS):
            sto_ref[0, h] = st_s[h].T


def _sample_kernel(x_ref, tab_ref, st0_ref, cmk_ref, cmv_ref, cwk_ref, cwv_ref, sinks_ref, lbp_ref, g1_ref,
                   hgg_ref, ang_ref, n2_ref, fn_ref, w_in_ref, w_out_ref, w_fi_ref, w_fo_ref,
                   y_ref, nk_ref, nv_ref, sto_ref,
                   q_s, k_s, v_s, b_s, g_s, aq_s, ohg_s, oatt_s, st_s, ks_s, vs_s, mk_s, mv_s, *, tile):
    for h in range(HG_HEADS):
        st_s[h] = st0_ref[0, h].T
    for src, dst in ((cmk_ref, mk_s), (cmv_ref, mv_s)):
        d0, d1 = _dup_heads(src[0])
        dst[0], dst[1] = d0.astype(BF16), d1.astype(BF16)
    for src, dst in ((cwk_ref, ks_s), (cwv_ref, vs_s)):
        d0, d1 = _dup_heads(src[0])
        dst[0, 0:WINDOW, :], dst[1, 0:WINDOW, :] = d0.astype(BF16), d1.astype(BF16)

    x = x_ref[0]
    lb = _forget_lower_bound(lbp_ref)
    ak, av = _project(x, tab_ref[...], lb, g1_ref[...], w_in_ref, q_s, k_s, v_s, b_s, g_s, aq_s, tile)
    _store_window(ak, av, ks_s, vs_s, tile)
    _run(_hgrn_steps(q_s, k_s, v_s, b_s, st_s, ohg_s, 0, tile),
         _attn_steps(aq_s, ks_s, vs_s, mk_s, mv_s, sinks_ref, oatt_s, 0, tile, 0, WINDOW + tile, 0))
    h1 = _mix_out(x, ohg_s, g_s, oatt_s, hgg_ref[...], ang_ref[...], w_out_ref)
    hn2 = _rms(h1, n2_ref[...]).astype(BF16)
    ffn_out = []
    _run(_ffn_steps(h1, hn2, w_fi_ref, w_fo_ref, 1, ffn_out))
    y_ref[0] = _rms(ffn_out[0], fn_ref[...])
    nk_ref[0] = ak
    nv_ref[0] = av
    for h in range(HG_HEADS):
        sto_ref[0, h] = st_s[h].T


def _rope_table(pos):
    inv = ROPE_THETA ** (-jnp.arange(0, ATT_HEAD_DIM, 2, dtype=F32) / ATT_HEAD_DIM)
    ang = pos.astype(F32)[:, None] * inv[None, :]
    ang = jnp.concatenate([ang, ang, ang, ang], axis=-1)
    first = (jnp.arange(LANES) % ATT_HEAD_DIM) < ATT_HEAD_DIM // 2
    cos, sin = jnp.cos(ang), jnp.sin(ang)
    return jnp.concatenate([cos, jnp.where(first, -sin, 0.0), jnp.where(first, 0.0, sin)], axis=-1)


def _vmem():
    return pl.BlockSpec(memory_space=pltpu.VMEM)


def _tile_scratch(tile):
    wide = lambda: pltpu.VMEM((tile, HG_WIDTH), F32)
    return [wide() for _ in range(8)] + [
        pltpu.VMEM((HG_HEADS, HG_DIM, HG_DIM), F32),
        pltpu.VMEM((N_KV_HEADS, WINDOW + tile, LANES), BF16),
        pltpu.VMEM((N_KV_HEADS, WINDOW + tile, LANES), BF16),
    ]


def _prompt_tile(seq):
    for t in (256, 128):
        if seq % t == 0:
            return t
    raise ValueError(f"sequence length {seq} must be a multiple of 128")


def kernel(x_prompt, x_sample, cache_meta_k, cache_meta_v, cache_win_k, cache_win_v, state_hgrn, meta_tokens,
           norm1, w_in, lb_param, hg_norm, attn_sinks, attn_norm, w_out, norm2, w_ffn_in, w_ffn_out, final_norm):
    B, S, D = x_prompt.shape
    Bd, T, _ = x_sample.shape
    assert norm1.shape[0] == 1 and lb_param.shape[0] == 2, "one-layer model"
    assert D == D_MODEL and T % SUB_BLOCK == 0 and T <= CHUNK and cache_win_k.shape[2] == WINDOW
    tile = _prompt_tile(S)

    w_in_b, w_out_b = w_in[0].astype(BF16), w_out[0].astype(BF16)
    w_fi_b, w_fo_b = w_ffn_in[0].astype(BF16), w_ffn_out[0].astype(BF16)
    row = lambda a: a.reshape(1, -1).astype(F32)
    g1, hgg, ang, n2, fn = row(norm1[0]), row(hg_norm[0]), row(attn_norm[0]), row(norm2[0]), row(final_norm)
    lbp = lb_param.astype(F32)
    sinks = attn_sinks[0].astype(F32)
    tab_m = _rope_table(jnp.arange(N_META))
    tab_p = _rope_table(N_META + jnp.arange(S))
    tab_s = _rope_table(N_META + PAST_LEN + jnp.arange(T))
    params = pltpu.CompilerParams(vmem_limit_bytes=VMEM_LIMIT)

    mak, mav, mk2, mv2, stm = pl.pallas_call(
        _meta_kernel,
        out_shape=(jax.ShapeDtypeStruct((N_META, KV_WIDTH), F32), jax.ShapeDtypeStruct((N_META, KV_WIDTH), F32),
                   jax.ShapeDtypeStruct((N_KV_HEADS, N_META, LANES), BF16),
                   jax.ShapeDtypeStruct((N_KV_HEADS, N_META, LANES), BF16),
                   jax.ShapeDtypeStruct((HG_HEADS, HG_DIM, HG_DIM), F32)),
        in_specs=[_vmem()] * 5, out_specs=tuple(_vmem() for _ in range(5)),
        compiler_params=params, name="meta",
    )(meta_tokens.astype(F32), tab_m, lbp, g1, w_in_b)

    smem = pl.BlockSpec(memory_space=pltpu.SMEM)
    weights_specs = [_vmem()] * 4
    nj = S // tile
    n_tiles = B * nj
    mix_tile = lambda s: jnp.minimum(s, n_tiles - 1)
    out_tile = lambda s: jnp.maximum(s - 1, 0)
    wide = lambda: pltpu.VMEM((tile, HG_WIDTH), F32)
    slots = lambda w: pltpu.VMEM((2, tile, w), F32)
    y_p, wk, wv, st_p = pl.pallas_call(
        functools.partial(_prompt_kernel, tile=tile, tiles_per_stream=nj, n_tiles=n_tiles),
        grid=(n_tiles + 1,),
        out_shape=(jax.ShapeDtypeStruct((B, S, D), F32), jax.ShapeDtypeStruct((B, WINDOW, KV_WIDTH), F32),
                   jax.ShapeDtypeStruct((B, WINDOW, KV_WIDTH), F32),
                   jax.ShapeDtypeStruct((B, HG_HEADS, HG_DIM, HG_DIM), F32)),
        in_specs=[pl.BlockSpec((1, tile, D), lambda s: (mix_tile(s) // nj, mix_tile(s) % nj, 0)),
                  pl.BlockSpec((tile, 3 * LANES), lambda s: (mix_tile(s) % nj, 0)),
                  _vmem(), _vmem(), _vmem(), smem] + [_vmem()] * 6 + weights_specs,
        out_specs=(pl.BlockSpec((1, tile, D), lambda s: (out_tile(s) // nj, out_tile(s) % nj, 0)),
                   pl.BlockSpec((1, WINDOW, KV_WIDTH), lambda s: (mix_tile(s) // nj, 0, 0)),
                   pl.BlockSpec((1, WINDOW, KV_WIDTH), lambda s: (mix_tile(s) // nj, 0, 0)),
                   pl.BlockSpec((1, HG_HEADS, HG_DIM, HG_DIM), lambda s: (mix_tile(s) // nj, 0, 0, 0))),
        scratch_shapes=[wide() for _ in range(5)] + [
            pltpu.VMEM((HG_HEADS, HG_DIM, HG_DIM), F32),
            pltpu.VMEM((N_KV_HEADS, WINDOW + tile, LANES), BF16),
            pltpu.VMEM((N_KV_HEADS, WINDOW + tile, LANES), BF16),
            slots(D_MODEL), slots(HG_WIDTH), slots(HG_WIDTH), slots(HG_WIDTH)],
        compiler_params=pltpu.CompilerParams(vmem_limit_bytes=VMEM_LIMIT, dimension_semantics=("arbitrary",)),
        name="prompt",
    )(x_prompt, tab_p, stm, mk2, mv2, sinks, lbp, g1, hgg, ang, n2, fn, w_in_b, w_out_b, w_fi_b, w_fo_b)

    per_stream3 = lambda n, w: pl.BlockSpec((1, n, w), lambda b: (b, 0, 0))
    state_spec = pl.BlockSpec((1, HG_HEADS, HG_DIM, HG_DIM), lambda b: (b, 0, 0, 0))
    y_s, nk, nv, st_s = pl.pallas_call(
        functools.partial(_sample_kernel, tile=T),
        grid=(Bd,),
        out_shape=(jax.ShapeDtypeStruct((Bd, T, D), F32), jax.ShapeDtypeStruct((Bd, T, KV_WIDTH), F32),
                   jax.ShapeDtypeStruct((Bd, T, KV_WIDTH), F32),
                   jax.ShapeDtypeStruct((Bd, HG_HEADS, HG_DIM, HG_DIM), F32)),
        in_specs=[per_stream3(T, D), _vmem(), state_spec,
                  per_stream3(N_META, KV_WIDTH), per_stream3(N_META, KV_WIDTH),
                  per_stream3(WINDOW, KV_WIDTH), per_stream3(WINDOW, KV_WIDTH),
                  smem] + [_vmem()] * 6 + weights_specs,
        out_specs=(per_stream3(T, D), per_stream3(T, KV_WIDTH), per_stream3(T, KV_WIDTH), state_spec),
        scratch_shapes=_tile_scratch(T) + [pltpu.VMEM((N_KV_HEADS, N_META, LANES), BF16),
                                           pltpu.VMEM((N_KV_HEADS, N_META, LANES), BF16)],
        compiler_params=pltpu.CompilerParams(vmem_limit_bytes=VMEM_LIMIT, dimension_semantics=("arbitrary",)),
        name="sample",
    )(x_sample, tab_s, state_hgrn[0].astype(F32),
      cache_meta_k[0].reshape(Bd, N_META, KV_WIDTH), cache_meta_v[0].reshape(Bd, N_META, KV_WIDTH),
      cache_win_k[0].reshape(Bd, WINDOW, KV_WIDTH), cache_win_v[0].reshape(Bd, WINDOW, KV_WIDTH),
      sinks, lbp, g1, hgg, ang, n2, fn, w_in_b, w_out_b, w_fi_b, w_fo_b)

    kv5 = lambda a, n, t: a.reshape(1, n, t, N_KV_HEADS, ATT_HEAD_DIM)
    bmeta = lambda a: jnp.broadcast_to(a.reshape(1, 1, N_META, N_KV_HEADS, ATT_HEAD_DIM),
                                       (1, B, N_META, N_KV_HEADS, ATT_HEAD_DIM))
    return (y_p, y_s, bmeta(mak), bmeta(mav), kv5(wk, B, WINDOW), kv5(wv, B, WINDOW), st_p[None],
            kv5(nk, Bd, T), kv5(nv, Bd, T), st_s[None])
```

```python
import functools

import jax
import jax.numpy as jnp
from jax import lax
from jax.experimental import pallas as pl
from jax.experimental.pallas import tpu as pltpu

F32 = jnp.float32
BF16 = jnp.bfloat16

D_MODEL = 1024
CHUNK = 64
N_META = 16
HG_WIDTH = 512
HG_HEADS = 4
HG_DIM = 128
ATT_WIDTH = 512
ATT_HEAD_DIM = 64
N_KV_HEADS = 2
Q_PER_KV = 4
KV_WIDTH = 128
WINDOW = 128
PAST_LEN = 1024
ROPE_THETA = 10000.0
D_FF = 2816
EPS = 1e-6

COL_Q, COL_F, COL_I, COL_G = 0, HG_WIDTH, 2 * HG_WIDTH, 3 * HG_WIDTH
COL_AQ = 4 * HG_WIDTH
COL_AK = COL_AQ + ATT_WIDTH
COL_AV = COL_AK + KV_WIDTH

LANES = 128
SUB_BLOCK = 16
MXU_TILE = 256
KEY_PAD = MXU_TILE
NEG_BIG = -1e30
VMEM_LIMIT = 56 * 1024 * 1024

NT_DIMS = (((1,), (1,)), ((), ()))
TN_DIMS = (((0,), (0,)), ((), ()))


def _rms(x, g):
    return x * lax.rsqrt(jnp.mean(x * x, axis=-1, keepdims=True) + EPS) * g


def _sigmoid(x):
    return 1.0 / (1.0 + jnp.exp(-x))


def _silu(x):
    return x * _sigmoid(x)


def _dot(a, b):
    return jnp.dot(a, b, preferred_element_type=F32)


def _forget_lower_bound(lbp_ref):
    p = lbp_ref[...]
    m = jnp.max(p, axis=0, keepdims=True)
    e = jnp.exp(p - m)
    return e[0:1] / jnp.sum(e, axis=0, keepdims=True)


def _cumsum_blocks(x, block):
    t = x.shape[0]
    r = lax.broadcasted_iota(jnp.int32, (t, t), 0)
    c = lax.broadcasted_iota(jnp.int32, (t, t), 1)
    d = r - c
    tri = jnp.where(d >= 0, jnp.where(d <= (r & (block - 1)), 1.0, 0.0), 0.0).astype(BF16)
    a1 = x.astype(BF16)
    r1 = x - a1.astype(F32)
    a2 = r1.astype(BF16)
    a3 = (r1 - a2.astype(F32)).astype(BF16)
    return _dot(tri, a1) + _dot(tri, a2) + _dot(tri, a3)


def _rope(x, tab):
    cos, sin_a, sin_b = tab[:, 0:LANES], tab[:, LANES:2 * LANES], tab[:, 2 * LANES:3 * LANES]
    return x * cos + pltpu.roll(x, 96, 1) * sin_a + pltpu.roll(x, 32, 1) * sin_b


def _dup_heads(x):
    lo = lax.broadcasted_iota(jnp.int32, (1, LANES), 1) < ATT_HEAD_DIM
    sw = pltpu.roll(x, ATT_HEAD_DIM, 1)
    return jnp.where(lo, x, sw), jnp.where(lo, sw, x)


def _project_steps(x, tab, lb, g1, w_in_ref, q_ref, k_ref, v_ref, b_ref, g_ref, aq_ref, block, out):
    hn = _rms(x, g1).astype(BF16)
    col = lambda c0, w: _dot(hn, w_in_ref[:, c0:c0 + w])
    zf, zq, zg, zi = col(COL_F, HG_WIDTH), col(COL_Q, HG_WIDTH), col(COL_G, HG_WIDTH), col(COL_I, HG_WIDTH)
    aq, akv = col(COL_AQ, ATT_WIDTH), col(COL_AK, 2 * KV_WIDTH)
    yield
    f = lb + (1.0 - lb) * _sigmoid(zf)
    b_ref[...] = _cumsum_blocks(jnp.log(f), block)
    k_ref[...] = 1.0 - f
    g_ref[...] = zg
    q_ref[...] = _silu(zq)
    v_ref[...] = zi
    for j in range(ATT_WIDTH // LANES):
        ls = slice(j * LANES, (j + 1) * LANES)
        aq_ref[:, ls] = _rope(aq[:, ls], tab) * (ATT_HEAD_DIM ** -0.5)
    out += [_rope(akv[:, 0:KV_WIDTH], tab), akv[:, KV_WIDTH:2 * KV_WIDTH]]


def _mixer_steps(q_ref, k_ref, v_ref, b_ref, st_ref, ohg_ref, aq_ref, ks_ref, vs_ref, mk_ref, mv_ref, sinks_ref,
                 oatt_ref, r0, length, kb0, nband, first_valid):
    nb = length // SUB_BLOCK
    nq = length
    row8 = lax.broadcasted_iota(jnp.int32, (8, 1), 0)
    rowl = lax.broadcasted_iota(jnp.int32, (length, 1), 0)
    lo = lax.broadcasted_iota(jnp.int32, (1, LANES), 1) < ATT_HEAD_DIM
    heads = [slice(h * HG_DIM, (h + 1) * HG_DIM) for h in range(HG_HEADS)]
    rows = lambda ref, ls, a, n: ref[r0 + a:r0 + a + n, ls]

    o_inter, st_inc, a_off, scores = [], [], [], []
    for ls, h in zip(heads, range(HG_HEADS)):
        q, k, v, b = (rows(r, ls, 0, length) for r in (q_ref, k_ref, v_ref, b_ref))
        b_last = rows(b_ref, ls, length - 1, 1)
        o_inter.append(lax.dot_general((q * jnp.exp(b)).astype(BF16), st_ref[h].astype(BF16), NT_DIMS,
                                       preferred_element_type=F32))
        k_end = (k * jnp.exp(b_last - b)).astype(BF16)
        st_inc.append(lax.dot_general(v.astype(BF16), k_end, TN_DIMS, preferred_element_type=F32))
        blocks = [jnp.zeros((SUB_BLOCK, length), F32)]
        for i in range(1, nb):
            c_i = rows(b_ref, ls, SUB_BLOCK * i - 1, 1)
            rs = slice(SUB_BLOCK * i, SUB_BLOCK * (i + 1))
            q_i = (q[rs] * jnp.exp(b[rs] - c_i)).astype(BF16)
            k_i = jnp.where(rowl < SUB_BLOCK * i, k * jnp.exp(jnp.minimum(c_i - b, 0.0)), 0.0).astype(BF16)
            blocks.append(lax.dot_general(q_i, k_i, NT_DIMS, preferred_element_type=F32))
        a_off.append(blocks)
    pad = jnp.zeros((KEY_PAD - nband - N_META, LANES), BF16)
    for g in range(N_KV_HEADS):
        keys = jnp.concatenate([ks_ref[g, kb0:kb0 + nband, :], mk_ref[g], pad], axis=0)
        qs = []
        for j in range(2):
            c0 = (2 * g + j) * LANES
            qj = aq_ref[r0:r0 + nq, c0:c0 + LANES]
            qs += [jnp.where(lo, qj, 0.0), jnp.where(lo, 0.0, qj)]
        qst = jnp.concatenate(qs, axis=0).astype(BF16)
        scores.append(lax.dot_general(qst, keys, NT_DIMS, preferred_element_type=F32))
    yield

    for ls in heads:
        for g8 in range(length // 8):
            t0 = 8 * g8
            blk0 = (t0 // SUB_BLOCK) * SUB_BLOCK
            q8, b8 = rows(q_ref, ls, t0, 8), rows(b_ref, ls, t0, 8)
            acc = jnp.zeros((8, HG_DIM), F32)
            for s in range(blk0, t0 + 8):
                b_s, k_s, v_s = rows(b_ref, ls, s, 1), rows(k_ref, ls, s, 1), rows(v_ref, ls, s, 1)
                w = jnp.sum(q8 * k_s * jnp.exp(b8 - b_s), axis=-1, keepdims=True)
                if s >= t0:
                    w = jnp.where(row8 >= s - t0, w, 0.0)
                acc = acc + w * v_s
            ohg_ref[r0 + t0:r0 + t0 + 8, ls] = acc
    yield

    for ls, h in zip(heads, range(HG_HEADS)):
        b_last = rows(b_ref, ls, length - 1, 1)
        st_ref[h] = st_ref[h] * jnp.exp(b_last) + st_inc[h]
        o = rows(ohg_ref, ls, 0, length) + o_inter[h]
        if nb > 1:
            o = o + _dot(jnp.concatenate(a_off[h], axis=0).astype(BF16), rows(v_ref, ls, 0, length).astype(BF16))
        ohg_ref[r0:r0 + length, ls] = o
    kidx = lax.broadcasted_iota(jnp.int32, (1, KEY_PAD), 1)
    valid = jnp.logical_and(kidx >= first_valid, kidx < nband + N_META)
    rid = lax.broadcasted_iota(jnp.int32, (Q_PER_KV * nq, 1), 0)
    for g in range(N_KV_HEADS):
        vals = jnp.concatenate([vs_ref[g, kb0:kb0 + nband, :], mv_ref[g], pad], axis=0)
        s = jnp.where(valid, scores[g], NEG_BIG)
        sink = jnp.where(rid < nq, sinks_ref[g, 0],
                         jnp.where(rid < 2 * nq, sinks_ref[g, 1],
                                   jnp.where(rid < 3 * nq, sinks_ref[g, 2], sinks_ref[g, 3])))
        m = jnp.maximum(jnp.max(s, axis=-1, keepdims=True), sink)
        p = jnp.exp(s - m)
        den = jnp.sum(p, axis=-1, keepdims=True) + jnp.exp(sink - m)
        pv = _dot(p.astype(BF16), vals) / den
        for j in range(2):
            c0 = (2 * g + j) * LANES
            oatt_ref[r0:r0 + nq, c0:c0 + LANES] = jnp.where(
                lo, pv[(2 * j) * nq:(2 * j + 1) * nq], pv[(2 * j + 1) * nq:(2 * j + 2) * nq])
    yield


def _ffn_steps(acc, hn2, w_fi_ref, w_fo_ref, bounds, out):
    for c0, c1 in bounds:
        gate = _dot(hn2, w_fi_ref[:, c0:c1])
        yield
        up = _dot(hn2, w_fi_ref[:, D_FF + c0:D_FF + c1])
        yield
        acc = acc + _dot((_silu(gate) * up).astype(BF16), w_fo_ref[c0:c1, :])
        yield
    out.append(acc)


def _run(*gens):
    for g in gens:
        for _ in g:
            pass


def _mix_out(x, ohg_ref, g_ref, oatt_ref, hgg, ang, w_out_ref):
    parts = []
    for h in range(HG_HEADS):
        ls = slice(h * HG_DIM, (h + 1) * HG_DIM)
        o = ohg_ref[:, ls]
        on = o * lax.rsqrt(jnp.mean(o * o, axis=-1, keepdims=True) + EPS)
        parts.append(on * hgg[:, ls] * _silu(g_ref[:, ls]))
    parts.append(_rms(oatt_ref[...], ang))
    mixed = _dot(jnp.concatenate(parts, axis=-1).astype(BF16), w_out_ref[...])
    return x + mixed


def _ffn_bounds(parts):
    tiles = D_FF // MXU_TILE
    cuts = [MXU_TILE * ((tiles * p + parts - 1) // parts) for p in range(parts + 1)]
    return list(zip(cuts[:-1], cuts[1:]))


def _store_window(ak, av, ks_ref, vs_ref, t):
    k0, k1 = _dup_heads(ak)
    v0, v1 = _dup_heads(av)
    ks_ref[0, WINDOW:WINDOW + t, :] = k0.astype(BF16)
    ks_ref[1, WINDOW:WINDOW + t, :] = k1.astype(BF16)
    vs_ref[0, WINDOW:WINDOW + t, :] = v0.astype(BF16)
    vs_ref[1, WINDOW:WINDOW + t, :] = v1.astype(BF16)


def _meta_kernel(x_ref, tab_ref, lbp_ref, g1_ref, w_in_ref,
                 mak_ref, mav_ref, mk2_ref, mv2_ref, st_ref):
    hn = _rms(x_ref[...], g1_ref[...]).astype(BF16)
    col = lambda c0, w: _dot(hn, w_in_ref[:, c0:c0 + w])
    lb = _forget_lower_bound(lbp_ref)
    f = lb + (1.0 - lb) * _sigmoid(col(COL_F, HG_WIDTH))
    b = _cumsum_blocks(jnp.log(f), N_META)
    k_end = ((1.0 - f) * jnp.exp(b[N_META - 1:N_META] - b)).astype(BF16)
    v16 = col(COL_I, HG_WIDTH).astype(BF16)
    for h in range(HG_HEADS):
        ls = slice(h * HG_DIM, (h + 1) * HG_DIM)
        st_ref[h] = lax.dot_general(v16[:, ls], k_end[:, ls], TN_DIMS, preferred_element_type=F32)
    ak = _rope(col(COL_AK, KV_WIDTH), tab_ref[...])
    av = col(COL_AV, KV_WIDTH)
    mak_ref[...] = ak
    mav_ref[...] = av
    k0, k1 = _dup_heads(ak)
    v0, v1 = _dup_heads(av)
    mk2_ref[0], mk2_ref[1] = k0.astype(BF16), k1.astype(BF16)
    mv2_ref[0], mv2_ref[1] = v0.astype(BF16), v1.astype(BF16)


def _prompt_kernel(x_ref, tab_ref, stm_ref, mk2_ref, mv2_ref, sinks_ref, lbp_ref, g1_ref, hgg_ref, ang_ref,
                   n2_ref, fn_ref, w_in_ref, w_out_ref, w_fi_ref, w_fo_ref,
                   y_ref, wk_ref, wv_ref, sto_ref,
                   q_s, k_s, v_s, b_s, aq_s, st_s, ks_s, vs_s, x2_s, g2_s, ohg2_s, oatt2_s, acc_s, hn2_s,
                   *, tile, tiles_per_stream, n_tiles):
    s = pl.program_id(0)
    cur = lax.rem(s, 2)
    prev = 1 - cur
    j = lax.rem(jnp.minimum(s, n_tiles - 1), tiles_per_stream)

    @pl.when(s == 0)
    def _():
        for ref in (g2_s, ohg2_s, oatt2_s):
            ref[1] = jnp.zeros((tile, HG_WIDTH), F32)
        x2_s[1] = jnp.zeros((tile, D_MODEL), F32)

    @pl.when(j == 0)
    def _():
        st_s[...] = stm_ref[...]
        ks_s[:, 0:WINDOW, :] = jnp.zeros((N_KV_HEADS, WINDOW, LANES), BF16)
        vs_s[:, 0:WINDOW, :] = jnp.zeros((N_KV_HEADS, WINDOW, LANES), BF16)

    once = jnp.minimum(s, 0) + 1
    region = lambda body: lax.fori_loop(0, once, lambda _, carry: (body(), carry)[1], 0)
    n_chunks = tile // CHUNK

    @region
    def _():
        x = x_ref[0]
        x2_s[cur] = x
        kv = []
        proj = _project_steps(x, tab_ref[...], _forget_lower_bound(lbp_ref), g1_ref[...], w_in_ref,
                              q_s, k_s, v_s, b_s, g2_s.at[cur], aq_s, CHUNK, kv)
        next(proj)
        h1 = _mix_out(x2_s[prev], ohg2_s.at[prev], g2_s.at[prev], oatt2_s.at[prev], hgg_ref[...], ang_ref[...],
                      w_out_ref)
        acc_s[...] = h1
        hn2_s[...] = _rms(h1, n2_ref[...]).astype(BF16)
        _run(proj)
        ak, av = kv
        _store_window(ak, av, ks_s, vs_s, tile)
        wk_ref[0] = ak[tile - WINDOW:tile]
        wv_ref[0] = av[tile - WINDOW:tile]

    for c, bounds in enumerate(_ffn_bounds(n_chunks)):
        @region
        def _(c=c, bounds=bounds):
            r0 = c * CHUNK
            first_valid = jnp.maximum(WINDOW - (j * n_chunks + c) * CHUNK, 0)
            ffn_out = []
            ffn = _ffn_steps(acc_s[...], hn2_s[...], w_fi_ref, w_fo_ref, [bounds], ffn_out)
            mix = _mixer_steps(q_s, k_s, v_s, b_s, st_s, ohg2_s.at[cur], aq_s, ks_s, vs_s, mk2_ref, mv2_ref,
                               sinks_ref, oatt2_s.at[cur], r0, CHUNK, r0, WINDOW + CHUNK, first_valid)
            for gen in (ffn, mix, mix, ffn, mix, ffn):
                next(gen)
            _run(ffn, mix)
            acc_s[...] = ffn_out[0]

    @region
    def _():
        y_ref[0] = _rms(acc_s[...], fn_ref[...])
        ks_s[:, 0:WINDOW, :] = ks_s[:, tile:tile + WINDOW, :]
        vs_s[:, 0:WINDOW, :] = vs_s[:, tile:tile + WINDOW, :]

    @pl.when(jnp.logical_and(j == tiles_per_stream - 1, s < n_tiles))
    def _():
        for h in range(HG_HEADS):
            sto_ref[0, h] = st_s[h].T


def _sample_kernel(x_ref, tab_ref, st0_ref, cmk_ref, cmv_ref, cwk_ref, cwv_ref, sinks_ref, lbp_ref, g1_ref,
                   hgg_ref, ang_ref, n2_ref, fn_ref, w_in_ref, w_out_ref, w_fi_ref, w_fo_ref,
                   y_ref, nk_ref, nv_ref, sto_ref,
                   q_s, k_s, v_s, b_s, g_s, aq_s, ohg_s, oatt_s, st_s, ks_s, vs_s, mk_s, mv_s, *, tile):
    for h in range(HG_HEADS):
        st_s[h] = st0_ref[0, h].T
    for src, dst in ((cmk_ref, mk_s), (cmv_ref, mv_s)):
        d0, d1 = _dup_heads(src[0])
        dst[0], dst[1] = d0.astype(BF16), d1.astype(BF16)
    for src, dst in ((cwk_ref, ks_s), (cwv_ref, vs_s)):
        d0, d1 = _dup_heads(src[0])
        dst[0, 0:WINDOW, :], dst[1, 0:WINDOW, :] = d0.astype(BF16), d1.astype(BF16)

    x = x_ref[0]
    lb = _forget_lower_bound(lbp_ref)
    kv = []
    _run(_project_steps(x, tab_ref[...], lb, g1_ref[...], w_in_ref, q_s, k_s, v_s, b_s, g_s, aq_s, tile, kv))
    ak, av = kv
    _store_window(ak, av, ks_s, vs_s, tile)
    _run(_mixer_steps(q_s, k_s, v_s, b_s, st_s, ohg_s, aq_s, ks_s, vs_s, mk_s, mv_s, sinks_ref, oatt_s,
                      0, tile, 0, WINDOW + tile, 0))
    h1 = _mix_out(x, ohg_s, g_s, oatt_s, hgg_ref[...], ang_ref[...], w_out_ref)
    hn2 = _rms(h1, n2_ref[...]).astype(BF16)
    ffn_out = []
    _run(_ffn_steps(h1, hn2, w_fi_ref, w_fo_ref, [(0, D_FF)], ffn_out))
    y_ref[0] = _rms(ffn_out[0], fn_ref[...])
    nk_ref[0] = ak
    nv_ref[0] = av
    for h in range(HG_HEADS):
        sto_ref[0, h] = st_s[h].T


def _rope_table(pos):
    inv = ROPE_THETA ** (-jnp.arange(0, ATT_HEAD_DIM, 2, dtype=F32) / ATT_HEAD_DIM)
    ang = pos.astype(F32)[:, None] * inv[None, :]
    ang = jnp.concatenate([ang, ang, ang, ang], axis=-1)
    first = (jnp.arange(LANES) % ATT_HEAD_DIM) < ATT_HEAD_DIM // 2
    cos, sin = jnp.cos(ang), jnp.sin(ang)
    return jnp.concatenate([cos, jnp.where(first, -sin, 0.0), jnp.where(first, 0.0, sin)], axis=-1)


def _vmem():
    return pl.BlockSpec(memory_space=pltpu.VMEM)


def _tile_scratch(tile):
    wide = lambda: pltpu.VMEM((tile, HG_WIDTH), F32)
    return [wide() for _ in range(8)] + [
        pltpu.VMEM((HG_HEADS, HG_DIM, HG_DIM), F32),
        pltpu.VMEM((N_KV_HEADS, WINDOW + tile, LANES), BF16),
        pltpu.VMEM((N_KV_HEADS, WINDOW + tile, LANES), BF16),
    ]


def _prompt_tile(seq):
    for t in (256, 128):
        if seq % t == 0:
            return t
    raise ValueError(f"sequence length {seq} must be a multiple of 128")


def kernel(x_prompt, x_sample, cache_meta_k, cache_meta_v, cache_win_k, cache_win_v, state_hgrn, meta_tokens,
           norm1, w_in, lb_param, hg_norm, attn_sinks, attn_norm, w_out, norm2, w_ffn_in, w_ffn_out, final_norm):
    B, S, D = x_prompt.shape
    Bd, T, _ = x_sample.shape
    assert norm1.shape[0] == 1 and lb_param.shape[0] == 2, "one-layer model"
    assert D == D_MODEL and T % SUB_BLOCK == 0 and T <= CHUNK and cache_win_k.shape[2] == WINDOW
    tile = _prompt_tile(S)

    w_in_b, w_out_b = w_in[0].astype(BF16), w_out[0].astype(BF16)
    w_fi_b, w_fo_b = w_ffn_in[0].astype(BF16), w_ffn_out[0].astype(BF16)
    row = lambda a: a.reshape(1, -1).astype(F32)
    g1, hgg, ang, n2, fn = row(norm1[0]), row(hg_norm[0]), row(attn_norm[0]), row(norm2[0]), row(final_norm)
    lbp = lb_param.astype(F32)
    sinks = attn_sinks[0].astype(F32)
    tab_m = _rope_table(jnp.arange(N_META))
    tab_p = _rope_table(N_META + jnp.arange(S))
    tab_s = _rope_table(N_META + PAST_LEN + jnp.arange(T))
    params = pltpu.CompilerParams(vmem_limit_bytes=VMEM_LIMIT)

    mak, mav, mk2, mv2, stm = pl.pallas_call(
        _meta_kernel,
        out_shape=(jax.ShapeDtypeStruct((N_META, KV_WIDTH), F32), jax.ShapeDtypeStruct((N_META, KV_WIDTH), F32),
                   jax.ShapeDtypeStruct((N_KV_HEADS, N_META, LANES), BF16),
                   jax.ShapeDtypeStruct((N_KV_HEADS, N_META, LANES), BF16),
                   jax.ShapeDtypeStruct((HG_HEADS, HG_DIM, HG_DIM), F32)),
        in_specs=[_vmem()] * 5, out_specs=tuple(_vmem() for _ in range(5)),
        compiler_params=params, name="meta",
    )(meta_tokens.astype(F32), tab_m, lbp, g1, w_in_b)

    smem = pl.BlockSpec(memory_space=pltpu.SMEM)
    weights_specs = [_vmem()] * 4
    nj = S // tile
    n_tiles = B * nj
    mix_tile = lambda s: jnp.minimum(s, n_tiles - 1)
    out_tile = lambda s: jnp.maximum(s - 1, 0)
    wide = lambda: pltpu.VMEM((tile, HG_WIDTH), F32)
    slots = lambda w: pltpu.VMEM((2, tile, w), F32)
    y_p, wk, wv, st_p = pl.pallas_call(
        functools.partial(_prompt_kernel, tile=tile, tiles_per_stream=nj, n_tiles=n_tiles),
        grid=(n_tiles + 1,),
        out_shape=(jax.ShapeDtypeStruct((B, S, D), F32), jax.ShapeDtypeStruct((B, WINDOW, KV_WIDTH), F32),
                   jax.ShapeDtypeStruct((B, WINDOW, KV_WIDTH), F32),
                   jax.ShapeDtypeStruct((B, HG_HEADS, HG_DIM, HG_DIM), F32)),
        in_specs=[pl.BlockSpec((1, tile, D), lambda s: (mix_tile(s) // nj, mix_tile(s) % nj, 0)),
                  pl.BlockSpec((tile, 3 * LANES), lambda s: (mix_tile(s) % nj, 0)),
                  _vmem(), _vmem(), _vmem(), smem] + [_vmem()] * 6 + weights_specs,
        out_specs=(pl.BlockSpec((1, tile, D), lambda s: (out_tile(s) // nj, out_tile(s) % nj, 0)),
                   pl.BlockSpec((1, WINDOW, KV_WIDTH), lambda s: (mix_tile(s) // nj, 0, 0)),
                   pl.BlockSpec((1, WINDOW, KV_WIDTH), lambda s: (mix_tile(s) // nj, 0, 0)),
                   pl.BlockSpec((1, HG_HEADS, HG_DIM, HG_DIM), lambda s: (mix_tile(s) // nj, 0, 0, 0))),
        scratch_shapes=[wide() for _ in range(5)] + [
            pltpu.VMEM((HG_HEADS, HG_DIM, HG_DIM), F32),
            pltpu.VMEM((N_KV_HEADS, WINDOW + tile, LANES), BF16),
            pltpu.VMEM((N_KV_HEADS, WINDOW + tile, LANES), BF16),
            slots(D_MODEL), slots(HG_WIDTH), slots(HG_WIDTH), slots(HG_WIDTH),
            pltpu.VMEM((tile, D_MODEL), F32), pltpu.VMEM((tile, D_MODEL), BF16)],
        compiler_params=pltpu.CompilerParams(vmem_limit_bytes=VMEM_LIMIT, dimension_semantics=("arbitrary",)),
        name="prompt",
    )(x_prompt, tab_p, stm, mk2, mv2, sinks, lbp, g1, hgg, ang, n2, fn, w_in_b, w_out_b, w_fi_b, w_fo_b)

    per_stream3 = lambda n, w: pl.BlockSpec((1, n, w), lambda b: (b, 0, 0))
    state_spec = pl.BlockSpec((1, HG_HEADS, HG_DIM, HG_DIM), lambda b: (b, 0, 0, 0))
    y_s, nk, nv, st_s = pl.pallas_call(
        functools.partial(_sample_kernel, tile=T),
        grid=(Bd,),
        out_shape=(jax.ShapeDtypeStruct((Bd, T, D), F32), jax.ShapeDtypeStruct((Bd, T, KV_WIDTH), F32),
                   jax.ShapeDtypeStruct((Bd, T, KV_WIDTH), F32),
                   jax.ShapeDtypeStruct((Bd, HG_HEADS, HG_DIM, HG_DIM), F32)),
        in_specs=[per_stream3(T, D), _vmem(), state_spec,
                  per_stream3(N_META, KV_WIDTH), per_stream3(N_META, KV_WIDTH),
                  per_stream3(WINDOW, KV_WIDTH), per_stream3(WINDOW, KV_WIDTH),
                  smem] + [_vmem()] * 6 + weights_specs,
        out_specs=(per_stream3(T, D), per_stream3(T, KV_WIDTH), per_stream3(T, KV_WIDTH), state_spec),
        scratch_shapes=_tile_scratch(T) + [pltpu.VMEM((N_KV_HEADS, N_META, LANES), BF16),
                                           pltpu.VMEM((N_KV_HEADS, N_META, LANES), BF16)],
        compiler_params=pltpu.CompilerParams(vmem_limit_bytes=VMEM_LIMIT, dimension_semantics=("arbitrary",)),
        name="sample",
    )(x_sample, tab_s, state_hgrn[0].astype(F32),
      cache_meta_k[0].reshape(Bd, N_META, KV_WIDTH), cache_meta_v[0].reshape(Bd, N_META, KV_WIDTH),
      cache_win_k[0].reshape(Bd, WINDOW, KV_WIDTH), cache_win_v[0].reshape(Bd, WINDOW, KV_WIDTH),
      sinks, lbp, g1, hgg, ang, n2, fn, w_in_b, w_out_b, w_fi_b, w_fo_b)

    kv5 = lambda a, n, t: a.reshape(1, n, t, N_KV_HEADS, ATT_HEAD_DIM)
    bmeta = lambda a: jnp.broadcast_to(a.reshape(1, 1, N_META, N_KV_HEADS, ATT_HEAD_DIM),
                                       (1, B, N_META, N_KV_HEADS, ATT_HEAD_DIM))
    return (y_p, y_s, bmeta(mak), bmeta(mav), kv5(wk, B, WINDOW), kv5(wv, B, WINDOW), st_p[None],
            kv5(nk, Bd, T), kv5(nv, Bd, T), st_s[None])
```

```python
import functools

import jax
import jax.numpy as jnp
from jax import lax
from jax.experimental import pallas as pl
from jax.experimental.pallas import tpu as pltpu

F32 = jnp.float32
BF16 = jnp.bfloat16

D_MODEL = 1024
CHUNK = 64
N_META = 16
HG_WIDTH = 512
HG_HEADS = 4
HG_DIM = 128
ATT_WIDTH = 512
ATT_HEAD_DIM = 64
N_KV_HEADS = 2
Q_PER_KV = 4
KV_WIDTH = 128
WINDOW = 128
PAST_LEN = 1024
ROPE_THETA = 10000.0
D_FF = 2816
EPS = 1e-6

COL_Q, COL_F, COL_I, COL_G = 0, HG_WIDTH, 2 * HG_WIDTH, 3 * HG_WIDTH
COL_AQ = 4 * HG_WIDTH
COL_AK = COL_AQ + ATT_WIDTH
COL_AV = COL_AK + KV_WIDTH

LANES = 128
SUB_BLOCK = 16
MXU_TILE = 256
KEY_PAD = MXU_TILE
NEG_BIG = -1e30
VMEM_LIMIT = 56 * 1024 * 1024
REGION_GROUPS = (3, 3)

NT_DIMS = (((1,), (1,)), ((), ()))
TN_DIMS = (((0,), (0,)), ((), ()))


def _rms(x, g):
    return x * lax.rsqrt(jnp.mean(x * x, axis=-1, keepdims=True) + EPS) * g


def _sigmoid(x):
    return 0.5 * jnp.tanh(0.5 * x) + 0.5


def _silu(x):
    h = 0.5 * x
    return h + h * jnp.tanh(h)


def _dot(a, b):
    return jnp.dot(a, b, preferred_element_type=F32)


def _forget_lower_bound(lbp_ref):
    p = lbp_ref[...]
    m = jnp.max(p, axis=0, keepdims=True)
    e = jnp.exp(p - m)
    return e[0:1] / jnp.sum(e, axis=0, keepdims=True)


def _cumsum_blocks(x, block):
    t = x.shape[0]
    r = lax.broadcasted_iota(jnp.int32, (t, t), 0)
    c = lax.broadcasted_iota(jnp.int32, (t, t), 1)
    d = r - c
    tri = jnp.where(d >= 0, jnp.where(d <= (r & (block - 1)), 1.0, 0.0), 0.0).astype(BF16)
    a1 = x.astype(BF16)
    r1 = x - a1.astype(F32)
    a2 = r1.astype(BF16)
    a3 = (r1 - a2.astype(F32)).astype(BF16)
    return _dot(tri, a1) + _dot(tri, a2) + _dot(tri, a3)


def _rope(x, tab):
    cos, sin_a, sin_b = tab[:, 0:LANES], tab[:, LANES:2 * LANES], tab[:, 2 * LANES:3 * LANES]
    return x * cos + pltpu.roll(x, 96, 1) * sin_a + pltpu.roll(x, 32, 1) * sin_b


def _dup_heads(x):
    lo = lax.broadcasted_iota(jnp.int32, (1, LANES), 1) < ATT_HEAD_DIM
    sw = pltpu.roll(x, ATT_HEAD_DIM, 1)
    return jnp.where(lo, x, sw), jnp.where(lo, sw, x)


def _project_steps(x, tab, lb, g1, w_in_ref, q_ref, k_ref, v_ref, b_ref, g_ref, aq_ref, block, out):
    hn = _rms(x, g1).astype(BF16)
    col = lambda c0, w: _dot(hn, w_in_ref[:, c0:c0 + w])
    zf, zq, zg, zi = col(COL_F, HG_WIDTH), col(COL_Q, HG_WIDTH), col(COL_G, HG_WIDTH), col(COL_I, HG_WIDTH)
    aq, akv = col(COL_AQ, ATT_WIDTH), col(COL_AK, 2 * KV_WIDTH)
    yield
    f = lb + (1.0 - lb) * _sigmoid(zf)
    b_ref[...] = _cumsum_blocks(jnp.log2(f), block)
    k_ref[...] = 1.0 - f
    g_ref[...] = zg
    q_ref[...] = _silu(zq)
    v_ref[...] = zi
    for j in range(ATT_WIDTH // LANES):
        ls = slice(j * LANES, (j + 1) * LANES)
        aq_ref[:, ls] = _rope(aq[:, ls], tab) * (ATT_HEAD_DIM ** -0.5)
    out += [_rope(akv[:, 0:KV_WIDTH], tab), akv[:, KV_WIDTH:2 * KV_WIDTH]]


def _mixer_steps(q_ref, k_ref, v_ref, b_ref, st_ref, ohg_ref, aq_ref, ks_ref, vs_ref, mk_ref, mv_ref, sinks_ref,
                 oatt_ref, r0, length, kb0, nband, first_valid):
    nb = length // SUB_BLOCK
    nq = length
    row8 = lax.broadcasted_iota(jnp.int32, (8, 1), 0)
    rowl = lax.broadcasted_iota(jnp.int32, (length, 1), 0)
    lo = lax.broadcasted_iota(jnp.int32, (1, LANES), 1) < ATT_HEAD_DIM
    heads = [slice(h * HG_DIM, (h + 1) * HG_DIM) for h in range(HG_HEADS)]
    rows = lambda ref, ls, a, n: ref[r0 + a:r0 + a + n, ls]

    o_inter, st_inc, a_off, scores = [], [], [], []
    for ls, h in zip(heads, range(HG_HEADS)):
        q, k, v, b = (rows(r, ls, 0, length) for r in (q_ref, k_ref, v_ref, b_ref))
        b_last = rows(b_ref, ls, length - 1, 1)
        o_inter.append(lax.dot_general((q * jnp.exp2(b)).astype(BF16), st_ref[h].astype(BF16), NT_DIMS,
                                       preferred_element_type=F32))
        k_end = (k * jnp.exp2(b_last - b)).astype(BF16)
        st_inc.append(lax.dot_general(v.astype(BF16), k_end, TN_DIMS, preferred_element_type=F32))
        blocks = [jnp.zeros((SUB_BLOCK, length), F32)]
        for i in range(1, nb):
            c_i = rows(b_ref, ls, SUB_BLOCK * i - 1, 1)
            rs = slice(SUB_BLOCK * i, SUB_BLOCK * (i + 1))
            q_i = (q[rs] * jnp.exp2(b[rs] - c_i)).astype(BF16)
            k_i = jnp.where(rowl < SUB_BLOCK * i, k * jnp.exp2(jnp.minimum(c_i - b, 0.0)), 0.0).astype(BF16)
            blocks.append(lax.dot_general(q_i, k_i, NT_DIMS, preferred_element_type=F32))
        a_off.append(blocks)
    pad = jnp.zeros((KEY_PAD - nband - N_META, LANES), BF16)
    for g in range(N_KV_HEADS):
        keys = jnp.concatenate([ks_ref[g, kb0:kb0 + nband, :], mk_ref[g], pad], axis=0)
        qs = []
        for j in range(2):
            c0 = (2 * g + j) * LANES
            qj = aq_ref[r0:r0 + nq, c0:c0 + LANES]
            qs += [jnp.where(lo, qj, 0.0), jnp.where(lo, 0.0, qj)]
        qst = jnp.concatenate(qs, axis=0).astype(BF16)
        scores.append(lax.dot_general(qst, keys, NT_DIMS, preferred_element_type=F32))
    yield

    lane = lax.broadcasted_iota(jnp.int32, (1, LANES), 1)
    a_diag = []
    for ls in heads:
        groups = []
        for g8 in range(length // 8):
            t0 = 8 * g8
            blk0 = (t0 // SUB_BLOCK) * SUB_BLOCK
            q8, b8 = rows(q_ref, ls, t0, 8), rows(b_ref, ls, t0, 8)
            acc = jnp.zeros((8, LANES), F32)
            for s in range(blk0, t0 + 8):
                b_s, k_s = rows(b_ref, ls, s, 1), rows(k_ref, ls, s, 1)
                w = jnp.sum(q8 * k_s * jnp.exp2(b8 - b_s), axis=-1, keepdims=True)
                acc = jnp.where(lane == s, w, acc)
            groups.append(jnp.where(lane <= t0 + row8, acc, 0.0)[:, 0:length])
        a_diag.append(jnp.concatenate(groups, axis=0))
    yield

    for ls, h in zip(heads, range(HG_HEADS)):
        b_last = rows(b_ref, ls, length - 1, 1)
        st_ref[h] = st_ref[h] * jnp.exp2(b_last) + st_inc[h]
        a = (jnp.concatenate(a_off[h], axis=0) + a_diag[h]).astype(BF16)
        ohg_ref[r0:r0 + length, ls] = o_inter[h] + _dot(a, rows(v_ref, ls, 0, length).astype(BF16))
    kidx = lax.broadcasted_iota(jnp.int32, (1, KEY_PAD), 1)
    valid = jnp.logical_and(kidx >= first_valid, kidx < nband + N_META)
    rid = lax.broadcasted_iota(jnp.int32, (Q_PER_KV * nq, 1), 0)
    for g in range(N_KV_HEADS):
        vals = jnp.concatenate([vs_ref[g, kb0:kb0 + nband, :], mv_ref[g], pad], axis=0)
        s = jnp.where(valid, scores[g], NEG_BIG)
        sink = jnp.where(rid < nq, sinks_ref[g, 0],
                         jnp.where(rid < 2 * nq, sinks_ref[g, 1],
                                   jnp.where(rid < 3 * nq, sinks_ref[g, 2], sinks_ref[g, 3])))
        m = jnp.maximum(jnp.max(s, axis=-1, keepdims=True), sink)
        p = jnp.exp(s - m)
        den = jnp.sum(p, axis=-1, keepdims=True) + jnp.exp(sink - m)
        pv = _dot(p.astype(BF16), vals) / den
        for j in range(2):
            c0 = (2 * g + j) * LANES
            oatt_ref[r0:r0 + nq, c0:c0 + LANES] = jnp.where(
                lo, pv[(2 * j) * nq:(2 * j + 1) * nq], pv[(2 * j + 1) * nq:(2 * j + 2) * nq])
    yield


def _ffn_steps(acc, hn2, w_fi_ref, w_fo_ref, bounds, out):
    for c0, c1 in bounds:
        gate = _dot(hn2, w_fi_ref[:, c0:c1])
        yield
        up = _dot(hn2, w_fi_ref[:, D_FF + c0:D_FF + c1])
        yield
        acc = acc + _dot((_silu(gate) * up).astype(BF16), w_fo_ref[c0:c1, :])
        yield
    out.append(acc)


def _run(*gens):
    for g in gens:
        for _ in g:
            pass


def _mix_out(x, ohg_ref, g_ref, oatt_ref, hgg, ang, w_out_ref):
    parts = []
    for h in range(HG_HEADS):
        ls = slice(h * HG_DIM, (h + 1) * HG_DIM)
        o = ohg_ref[:, ls]
        on = o * lax.rsqrt(jnp.mean(o * o, axis=-1, keepdims=True) + EPS)
        parts.append(on * hgg[:, ls] * _silu(g_ref[:, ls]))
    parts.append(_rms(oatt_ref[...], ang))
    mixed = _dot(jnp.concatenate(parts, axis=-1).astype(BF16), w_out_ref[...])
    return x + mixed


def _ffn_bounds(parts):
    tiles = D_FF // MXU_TILE
    cuts = [MXU_TILE * ((tiles * p) // parts) for p in range(parts + 1)]
    return list(zip(cuts[:-1], cuts[1:]))


def _store_window(ak, av, ks_ref, vs_ref, t):
    k0, k1 = _dup_heads(ak)
    v0, v1 = _dup_heads(av)
    ks_ref[0, WINDOW:WINDOW + t, :] = k0.astype(BF16)
    ks_ref[1, WINDOW:WINDOW + t, :] = k1.astype(BF16)
    vs_ref[0, WINDOW:WINDOW + t, :] = v0.astype(BF16)
    vs_ref[1, WINDOW:WINDOW + t, :] = v1.astype(BF16)


def _meta_kernel(x_ref, tab_ref, lbp_ref, g1_ref, w_in_ref,
                 mak_ref, mav_ref, mk2_ref, mv2_ref, st_ref):
    hn = _rms(x_ref[...], g1_ref[...]).astype(BF16)
    col = lambda c0, w: _dot(hn, w_in_ref[:, c0:c0 + w])
    lb = _forget_lower_bound(lbp_ref)
    f = lb + (1.0 - lb) * _sigmoid(col(COL_F, HG_WIDTH))
    b = _cumsum_blocks(jnp.log2(f), N_META)
    k_end = ((1.0 - f) * jnp.exp2(b[N_META - 1:N_META] - b)).astype(BF16)
    v16 = col(COL_I, HG_WIDTH).astype(BF16)
    for h in range(HG_HEADS):
        ls = slice(h * HG_DIM, (h + 1) * HG_DIM)
        st_ref[h] = lax.dot_general(v16[:, ls], k_end[:, ls], TN_DIMS, preferred_element_type=F32)
    ak = _rope(col(COL_AK, KV_WIDTH), tab_ref[...])
    av = col(COL_AV, KV_WIDTH)
    mak_ref[...] = ak
    mav_ref[...] = av
    k0, k1 = _dup_heads(ak)
    v0, v1 = _dup_heads(av)
    mk2_ref[0], mk2_ref[1] = k0.astype(BF16), k1.astype(BF16)
    mv2_ref[0], mv2_ref[1] = v0.astype(BF16), v1.astype(BF16)


def _prompt_kernel(x_ref, tab_ref, stm_ref, mk2_ref, mv2_ref, sinks_ref, lbp_ref, g1_ref, hgg_ref, ang_ref,
                   n2_ref, fn_ref, w_in_ref, w_out_ref, w_fi_ref, w_fo_ref,
                   y_ref, wk_ref, wv_ref, sto_ref,
                   q_s, k_s, v_s, b_s, aq_s, st_s, ks_s, vs_s, x2_s, g2_s, ohg2_s, oatt2_s, acc_s, hn2_s,
                   *, tile, tiles_per_stream, n_tiles):
    s = pl.program_id(0)
    cur = lax.rem(s, 2)
    prev = 1 - cur
    j = lax.rem(jnp.minimum(s, n_tiles - 1), tiles_per_stream)

    @pl.when(s == 0)
    def _():
        for ref in (g2_s, ohg2_s, oatt2_s):
            ref[1] = jnp.zeros((tile, HG_WIDTH), F32)
        x2_s[1] = jnp.zeros((tile, D_MODEL), F32)

    @pl.when(j == 0)
    def _():
        st_s[...] = stm_ref[...]
        ks_s[:, 0:WINDOW, :] = jnp.zeros((N_KV_HEADS, WINDOW, LANES), BF16)
        vs_s[:, 0:WINDOW, :] = jnp.zeros((N_KV_HEADS, WINDOW, LANES), BF16)

    once = jnp.minimum(s, 0) + 1
    n_chunks = tile // CHUNK

    def project_piece():
        x = x_ref[0]
        x2_s[cur] = x
        kv = []
        proj = _project_steps(x, tab_ref[...], _forget_lower_bound(lbp_ref), g1_ref[...], w_in_ref,
                              q_s, k_s, v_s, b_s, g2_s.at[cur], aq_s, CHUNK, kv)
        next(proj)
        h1 = _mix_out(x2_s[prev], ohg2_s.at[prev], g2_s.at[prev], oatt2_s.at[prev], hgg_ref[...], ang_ref[...],
                      w_out_ref)
        acc_s[...] = h1
        hn2_s[...] = _rms(h1, n2_ref[...]).astype(BF16)
        _run(proj)
        ak, av = kv
        _store_window(ak, av, ks_s, vs_s, tile)
        wk_ref[0] = ak[tile - WINDOW:tile]
        wv_ref[0] = av[tile - WINDOW:tile]

    def chunk_piece(c, bounds):
        r0 = c * CHUNK
        first_valid = jnp.maximum(WINDOW - (j * n_chunks + c) * CHUNK, 0)
        ffn_out = []
        ffn = _ffn_steps(acc_s[...], hn2_s[...], w_fi_ref, w_fo_ref, [bounds], ffn_out)
        mix = _mixer_steps(q_s, k_s, v_s, b_s, st_s, ohg2_s.at[cur], aq_s, ks_s, vs_s, mk2_ref, mv2_ref,
                           sinks_ref, oatt2_s.at[cur], r0, CHUNK, r0, WINDOW + CHUNK, first_valid)
        for gen in (ffn, mix, mix, ffn, mix, ffn):
            next(gen)
        _run(ffn, mix)
        acc_s[...] = ffn_out[0]

    def final_piece():
        y_ref[0] = _rms(acc_s[...], fn_ref[...])
        ks_s[:, 0:WINDOW, :] = ks_s[:, tile:tile + WINDOW, :]
        vs_s[:, 0:WINDOW, :] = vs_s[:, tile:tile + WINDOW, :]

    pieces = [project_piece]
    pieces += [functools.partial(chunk_piece, c, bounds) for c, bounds in enumerate(_ffn_bounds(n_chunks))]
    pieces += [final_piece]
    assert sum(REGION_GROUPS) == len(pieces)
    start = 0
    for size in REGION_GROUPS:
        group = pieces[start:start + size]
        start += size
        lax.fori_loop(0, once, lambda _, carry, group=group: ([p() for p in group], carry)[1], 0)

    @pl.when(jnp.logical_and(j == tiles_per_stream - 1, s < n_tiles))
    def _():
        for h in range(HG_HEADS):
            sto_ref[0, h] = st_s[h].T


def _sample_kernel(x_ref, tab_ref, st0_ref, cmk_ref, cmv_ref, cwk_ref, cwv_ref, sinks_ref, lbp_ref, g1_ref,
                   hgg_ref, ang_ref, n2_ref, fn_ref, w_in_ref, w_out_ref, w_fi_ref, w_fo_ref,
                   y_ref, nk_ref, nv_ref, sto_ref,
                   q_s, k_s, v_s, b_s, g_s, aq_s, ohg_s, oatt_s, st_s, ks_s, vs_s, mk_s, mv_s, *, tile):
    for h in range(HG_HEADS):
        st_s[h] = st0_ref[0, h].T
    for src, dst in ((cmk_ref, mk_s), (cmv_ref, mv_s)):
        d0, d1 = _dup_heads(src[0])
        dst[0], dst[1] = d0.astype(BF16), d1.astype(BF16)
    for src, dst in ((cwk_ref, ks_s), (cwv_ref, vs_s)):
        d0, d1 = _dup_heads(src[0])
        dst[0, 0:WINDOW, :], dst[1, 0:WINDOW, :] = d0.astype(BF16), d1.astype(BF16)

    x = x_ref[0]
    lb = _forget_lower_bound(lbp_ref)
    kv = []
    _run(_project_steps(x, tab_ref[...], lb, g1_ref[...], w_in_ref, q_s, k_s, v_s, b_s, g_s, aq_s, tile, kv))
    ak, av = kv
    _store_window(ak, av, ks_s, vs_s, tile)
    _run(_mixer_steps(q_s, k_s, v_s, b_s, st_s, ohg_s, aq_s, ks_s, vs_s, mk_s, mv_s, sinks_ref, oatt_s,
                      0, tile, 0, WINDOW + tile, 0))
    h1 = _mix_out(x, ohg_s, g_s, oatt_s, hgg_ref[...], ang_ref[...], w_out_ref)
    hn2 = _rms(h1, n2_ref[...]).astype(BF16)
    ffn_out = []
    _run(_ffn_steps(h1, hn2, w_fi_ref, w_fo_ref, [(0, D_FF)], ffn_out))
    y_ref[0] = _rms(ffn_out[0], fn_ref[...])
    nk_ref[0] = ak
    nv_ref[0] = av
    for h in range(HG_HEADS):
        sto_ref[0, h] = st_s[h].T


def _rope_table(pos):
    inv = ROPE_THETA ** (-jnp.arange(0, ATT_HEAD_DIM, 2, dtype=F32) / ATT_HEAD_DIM)
    ang = pos.astype(F32)[:, None] * inv[None, :]
    ang = jnp.concatenate([ang, ang, ang, ang], axis=-1)
    first = (jnp.arange(LANES) % ATT_HEAD_DIM) < ATT_HEAD_DIM // 2
    cos, sin = jnp.cos(ang), jnp.sin(ang)
    return jnp.concatenate([cos, jnp.where(first, -sin, 0.0), jnp.where(first, 0.0, sin)], axis=-1)


def _vmem():
    return pl.BlockSpec(memory_space=pltpu.VMEM)


def _tile_scratch(tile):
    wide = lambda: pltpu.VMEM((tile, HG_WIDTH), F32)
    return [wide() for _ in range(8)] + [
        pltpu.VMEM((HG_HEADS, HG_DIM, HG_DIM), F32),
        pltpu.VMEM((N_KV_HEADS, WINDOW + tile, LANES), BF16),
        pltpu.VMEM((N_KV_HEADS, WINDOW + tile, LANES), BF16),
    ]


def _prompt_tile(seq):
    for t in (256, 128):
        if seq % t == 0:
            return t
    raise ValueError(f"sequence length {seq} must be a multiple of 128")


def kernel(x_prompt, x_sample, cache_meta_k, cache_meta_v, cache_win_k, cache_win_v, state_hgrn, meta_tokens,
           norm1, w_in, lb_param, hg_norm, attn_sinks, attn_norm, w_out, norm2, w_ffn_in, w_ffn_out, final_norm):
    B, S, D = x_prompt.shape
    Bd, T, _ = x_sample.shape
    assert norm1.shape[0] == 1 and lb_param.shape[0] == 2, "one-layer model"
    assert D == D_MODEL and T % SUB_BLOCK == 0 and T <= CHUNK and cache_win_k.shape[2] == WINDOW
    tile = _prompt_tile(S)

    w_in_b, w_out_b = w_in[0].astype(BF16), w_out[0].astype(BF16)
    w_fi_b, w_fo_b = w_ffn_in[0].astype(BF16), w_ffn_out[0].astype(BF16)
    row = lambda a: a.reshape(1, -1).astype(F32)
    g1, hgg, ang, n2, fn = row(norm1[0]), row(hg_norm[0]), row(attn_norm[0]), row(norm2[0]), row(final_norm)
    lbp = lb_param.astype(F32)
    sinks = attn_sinks[0].astype(F32)
    tab_m = _rope_table(jnp.arange(N_META))
    tab_p = _rope_table(N_META + jnp.arange(S))
    tab_s = _rope_table(N_META + PAST_LEN + jnp.arange(T))
    params = pltpu.CompilerParams(vmem_limit_bytes=VMEM_LIMIT)

    mak, mav, mk2, mv2, stm = pl.pallas_call(
        _meta_kernel,
        out_shape=(jax.ShapeDtypeStruct((N_META, KV_WIDTH), F32), jax.ShapeDtypeStruct((N_META, KV_WIDTH), F32),
                   jax.ShapeDtypeStruct((N_KV_HEADS, N_META, LANES), BF16),
                   jax.ShapeDtypeStruct((N_KV_HEADS, N_META, LANES), BF16),
                   jax.ShapeDtypeStruct((HG_HEADS, HG_DIM, HG_DIM), F32)),
        in_specs=[_vmem()] * 5, out_specs=tuple(_vmem() for _ in range(5)),
        compiler_params=params, name="meta",
    )(meta_tokens.astype(F32), tab_m, lbp, g1, w_in_b)

    smem = pl.BlockSpec(memory_space=pltpu.SMEM)
    weights_specs = [_vmem()] * 4
    nj = S // tile
    n_tiles = B * nj
    mix_tile = lambda s: jnp.minimum(s, n_tiles - 1)
    out_tile = lambda s: jnp.maximum(s - 1, 0)
    wide = lambda: pltpu.VMEM((tile, HG_WIDTH), F32)
    slots = lambda w: pltpu.VMEM((2, tile, w), F32)
    y_p, wk, wv, st_p = pl.pallas_call(
        functools.partial(_prompt_kernel, tile=tile, tiles_per_stream=nj, n_tiles=n_tiles),
        grid=(n_tiles + 1,),
        out_shape=(jax.ShapeDtypeStruct((B, S, D), F32), jax.ShapeDtypeStruct((B, WINDOW, KV_WIDTH), F32),
                   jax.ShapeDtypeStruct((B, WINDOW, KV_WIDTH), F32),
                   jax.ShapeDtypeStruct((B, HG_HEADS, HG_DIM, HG_DIM), F32)),
        in_specs=[pl.BlockSpec((1, tile, D), lambda s: (mix_tile(s) // nj, mix_tile(s) % nj, 0)),
                  pl.BlockSpec((tile, 3 * LANES), lambda s: (mix_tile(s) % nj, 0)),
                  _vmem(), _vmem(), _vmem(), smem] + [_vmem()] * 6 + weights_specs,
        out_specs=(pl.BlockSpec((1, tile, D), lambda s: (out_tile(s) // nj, out_tile(s) % nj, 0)),
                   pl.BlockSpec((1, WINDOW, KV_WIDTH), lambda s: (mix_tile(s) // nj, 0, 0)),
                   pl.BlockSpec((1, WINDOW, KV_WIDTH), lambda s: (mix_tile(s) // nj, 0, 0)),
                   pl.BlockSpec((1, HG_HEADS, HG_DIM, HG_DIM), lambda s: (mix_tile(s) // nj, 0, 0, 0))),
        scratch_shapes=[wide() for _ in range(5)] + [
            pltpu.VMEM((HG_HEADS, HG_DIM, HG_DIM), F32),
            pltpu.VMEM((N_KV_HEADS, WINDOW + tile, LANES), BF16),
            pltpu.VMEM((N_KV_HEADS, WINDOW + tile, LANES), BF16),
            slots(D_MODEL), slots(HG_WIDTH), slots(HG_WIDTH), slots(HG_WIDTH),
            pltpu.VMEM((tile, D_MODEL), F32), pltpu.VMEM((tile, D_MODEL), BF16)],
        compiler_params=pltpu.CompilerParams(vmem_limit_bytes=VMEM_LIMIT, dimension_semantics=("arbitrary",)),
        name="prompt",
    )(x_prompt, tab_p, stm, mk2, mv2, sinks, lbp, g1, hgg, ang, n2, fn, w_in_b, w_out_b, w_fi_b, w_fo_b)

    per_stream3 = lambda n, w: pl.BlockSpec((1, n, w), lambda b: (b, 0, 0))
    state_spec = pl.BlockSpec((1, HG_HEADS, HG_DIM, HG_DIM), lambda b: (b, 0, 0, 0))
    y_s, nk, nv, st_s = pl.pallas_call(
        functools.partial(_sample_kernel, tile=T),
        grid=(Bd,),
        out_shape=(jax.ShapeDtypeStruct((Bd, T, D), F32), jax.ShapeDtypeStruct((Bd, T, KV_WIDTH), F32),
                   jax.ShapeDtypeStruct((Bd, T, KV_WIDTH), F32),
                   jax.ShapeDtypeStruct((Bd, HG_HEADS, HG_DIM, HG_DIM), F32)),
        in_specs=[per_stream3(T, D), _vmem(), state_spec,
                  per_stream3(N_META, KV_WIDTH), per_stream3(N_META, KV_WIDTH),
                  per_stream3(WINDOW, KV_WIDTH), per_stream3(WINDOW, KV_WIDTH),
                  smem] + [_vmem()] * 6 + weights_specs,
        out_specs=(per_stream3(T, D), per_stream3(T, KV_WIDTH), per_stream3(T, KV_WIDTH), state_spec),
        scratch_shapes=_tile_scratch(T) + [pltpu.VMEM((N_KV_HEADS, N_META, LANES), BF16),
                                           pltpu.VMEM((N_KV_HEADS, N_META, LANES), BF16)],
        compiler_params=pltpu.CompilerParams(vmem_limit_bytes=VMEM_LIMIT, dimension_semantics=("arbitrary",)),
        name="sample",
    )(x_sample, tab_s, state_hgrn[0].astype(F32),
      cache_meta_k[0].reshape(Bd, N_META, KV_WIDTH), cache_meta_v[0].reshape(Bd, N_META, KV_WIDTH),
      cache_win_k[0].reshape(Bd, WINDOW, KV_WIDTH), cache_win_v[0].reshape(Bd, WINDOW, KV_WIDTH),
      sinks, lbp, g1, hgg, ang, n2, fn, w_in_b, w_out_b, w_fi_b, w_fo_b)

    kv5 = lambda a, n, t: a.reshape(1, n, t, N_KV_HEADS, ATT_HEAD_DIM)
    bmeta = lambda a: jnp.broadcast_to(a.reshape(1, 1, N_META, N_KV_HEADS, ATT_HEAD_DIM),
                                       (1, B, N_META, N_KV_HEADS, ATT_HEAD_DIM))
    return (y_p, y_s, bmeta(mak), bmeta(mav), kv5(wk, B, WINDOW), kv5(wv, B, WINDOW), st_p[None],
            kv5(nk, Bd, T), kv5(nv, Bd, T), st_s[None])
```

```python
import functools

import jax
import jax.numpy as jnp
from jax import lax
from jax.experimental import pallas as pl
from jax.experimental.pallas import tpu as pltpu

F32 = jnp.float32
BF16 = jnp.bfloat16

D_MODEL = 1024
CHUNK = 64
N_META = 16
HG_WIDTH = 512
HG_HEADS = 4
HG_DIM = 128
ATT_WIDTH = 512
ATT_HEAD_DIM = 64
N_KV_HEADS = 2
Q_PER_KV = 4
KV_WIDTH = 128
WINDOW = 128
PAST_LEN = 1024
ROPE_THETA = 10000.0
D_FF = 2816
EPS = 1e-6

COL_Q, COL_F, COL_I, COL_G = 0, HG_WIDTH, 2 * HG_WIDTH, 3 * HG_WIDTH
COL_AQ = 4 * HG_WIDTH
COL_AK = COL_AQ + ATT_WIDTH
COL_AV = COL_AK + KV_WIDTH

LANES = 128
SUB_BLOCK = 16
MXU_TILE = 256
KEY_PAD = MXU_TILE
NEG_BIG = -1e30
VMEM_LIMIT = 56 * 1024 * 1024
REGION_GROUPS = (3, 3)

NT_DIMS = (((1,), (1,)), ((), ()))
TN_DIMS = (((0,), (0,)), ((), ()))


def _rms(x, g):
    return x * lax.rsqrt(jnp.mean(x * x, axis=-1, keepdims=True) + EPS) * g


def _sigmoid(x):
    return 1.0 / (1.0 + jnp.exp(-x))


def _silu(x):
    h = 0.5 * x
    return h + h * jnp.tanh(h)


def _dot(a, b):
    return jnp.dot(a, b, preferred_element_type=F32)


def _forget_lower_bound(lbp_ref):
    p = lbp_ref[...]
    m = jnp.max(p, axis=0, keepdims=True)
    e = jnp.exp(p - m)
    return e[0:1] / jnp.sum(e, axis=0, keepdims=True)


def _cumsum_blocks(x, block):
    t = x.shape[0]
    r = lax.broadcasted_iota(jnp.int32, (t, t), 0)
    c = lax.broadcasted_iota(jnp.int32, (t, t), 1)
    d = r - c
    tri = jnp.where(d >= 0, jnp.where(d <= (r & (block - 1)), 1.0, 0.0), 0.0).astype(BF16)
    a1 = x.astype(BF16)
    r1 = x - a1.astype(F32)
    a2 = r1.astype(BF16)
    a3 = (r1 - a2.astype(F32)).astype(BF16)
    return _dot(tri, a1) + _dot(tri, a2) + _dot(tri, a3)


def _rope(x, tab):
    cos, sin_a, sin_b = tab[:, 0:LANES], tab[:, LANES:2 * LANES], tab[:, 2 * LANES:3 * LANES]
    return x * cos + pltpu.roll(x, 96, 1) * sin_a + pltpu.roll(x, 32, 1) * sin_b


def _dup_heads(x):
    lo = lax.broadcasted_iota(jnp.int32, (1, LANES), 1) < ATT_HEAD_DIM
    sw = pltpu.roll(x, ATT_HEAD_DIM, 1)
    return jnp.where(lo, x, sw), jnp.where(lo, sw, x)


def _project_steps(x, tab, lb, g1, w_in_ref, q_ref, k_ref, v_ref, b_ref, g_ref, aq_ref, block, out):
    hn = _rms(x, g1).astype(BF16)
    col = lambda c0, w: _dot(hn, w_in_ref[:, c0:c0 + w])
    zf, zq, zg, zi = col(COL_F, HG_WIDTH), col(COL_Q, HG_WIDTH), col(COL_G, HG_WIDTH), col(COL_I, HG_WIDTH)
    aq, akv = col(COL_AQ, ATT_WIDTH), col(COL_AK, 2 * KV_WIDTH)
    yield
    f = lb + (1.0 - lb) * _sigmoid(zf)
    b_ref[...] = _cumsum_blocks(jnp.log2(f), block)
    k_ref[...] = 1.0 - f
    g_ref[...] = zg
    q_ref[...] = _silu(zq)
    v_ref[...] = zi
    for j in range(ATT_WIDTH // LANES):
        ls = slice(j * LANES, (j + 1) * LANES)
        aq_ref[:, ls] = _rope(aq[:, ls], tab) * (ATT_HEAD_DIM ** -0.5)
    out += [_rope(akv[:, 0:KV_WIDTH], tab), akv[:, KV_WIDTH:2 * KV_WIDTH]]


def _mixer_steps(q_ref, k_ref, v_ref, b_ref, st_ref, ohg_ref, aq_ref, ks_ref, vs_ref, mk_ref, mv_ref, sinks_ref,
                 oatt_ref, r0, length, kb0, nband, first_valid):
    nb = length // SUB_BLOCK
    nq = length
    row8 = lax.broadcasted_iota(jnp.int32, (8, 1), 0)
    rowl = lax.broadcasted_iota(jnp.int32, (length, 1), 0)
    lo = lax.broadcasted_iota(jnp.int32, (1, LANES), 1) < ATT_HEAD_DIM
    heads = [slice(h * HG_DIM, (h + 1) * HG_DIM) for h in range(HG_HEADS)]
    rows = lambda ref, ls, a, n: ref[r0 + a:r0 + a + n, ls]

    o_inter, st_inc, a_off, scores = [], [], [], []
    for ls, h in zip(heads, range(HG_HEADS)):
        q, k, v, b = (rows(r, ls, 0, length) for r in (q_ref, k_ref, v_ref, b_ref))
        b_last = rows(b_ref, ls, length - 1, 1)
        o_inter.append(lax.dot_general((q * jnp.exp2(b)).astype(BF16), st_ref[h].astype(BF16), NT_DIMS,
                                       preferred_element_type=F32))
        k_end = (k * jnp.exp2(b_last - b)).astype(BF16)
        st_inc.append(lax.dot_general(v.astype(BF16), k_end, TN_DIMS, preferred_element_type=F32))
        blocks = [jnp.zeros((SUB_BLOCK, length), F32)]
        for i in range(1, nb):
            c_i = rows(b_ref, ls, SUB_BLOCK * i - 1, 1)
            rs = slice(SUB_BLOCK * i, SUB_BLOCK * (i + 1))
            q_i = (q[rs] * jnp.exp2(b[rs] - c_i)).astype(BF16)
            k_i = jnp.where(rowl < SUB_BLOCK * i, k * jnp.exp2(jnp.minimum(c_i - b, 0.0)), 0.0).astype(BF16)
            blocks.append(lax.dot_general(q_i, k_i, NT_DIMS, preferred_element_type=F32))
        a_off.append(blocks)
    pad = jnp.zeros((KEY_PAD - nband - N_META, LANES), BF16)
    for g in range(N_KV_HEADS):
        keys = jnp.concatenate([ks_ref[g, kb0:kb0 + nband, :], mk_ref[g], pad], axis=0)
        qs = []
        for j in range(2):
            c0 = (2 * g + j) * LANES
            qj = aq_ref[r0:r0 + nq, c0:c0 + LANES]
            qs += [jnp.where(lo, qj, 0.0), jnp.where(lo, 0.0, qj)]
        qst = jnp.concatenate(qs, axis=0).astype(BF16)
        scores.append(lax.dot_general(qst, keys, NT_DIMS, preferred_element_type=F32))
    yield

    lane = lax.broadcasted_iota(jnp.int32, (1, LANES), 1)
    a_diag = []
    for ls in heads:
        groups = []
        for g8 in range(length // 8):
            t0 = 8 * g8
            blk0 = (t0 // SUB_BLOCK) * SUB_BLOCK
            q8, b8 = rows(q_ref, ls, t0, 8), rows(b_ref, ls, t0, 8)
            acc = jnp.zeros((8, LANES), F32)
            for s in range(blk0, t0 + 8):
                b_s, k_s = rows(b_ref, ls, s, 1), rows(k_ref, ls, s, 1)
                w = jnp.sum(q8 * k_s * jnp.exp2(b8 - b_s), axis=-1, keepdims=True)
                acc = jnp.where(lane == s, w, acc)
            groups.append(jnp.where(lane <= t0 + row8, acc, 0.0)[:, 0:length])
        a_diag.append(jnp.concatenate(groups, axis=0))
    yield

    for ls, h in zip(heads, range(HG_HEADS)):
        b_last = rows(b_ref, ls, length - 1, 1)
        st_ref[h] = st_ref[h] * jnp.exp2(b_last) + st_inc[h]
        a = (jnp.concatenate(a_off[h], axis=0) + a_diag[h]).astype(BF16)
        ohg_ref[r0:r0 + length, ls] = o_inter[h] + _dot(a, rows(v_ref, ls, 0, length).astype(BF16))
    kidx = lax.broadcasted_iota(jnp.int32, (1, KEY_PAD), 1)
    valid = jnp.logical_and(kidx >= first_valid, kidx < nband + N_META)
    rid = lax.broadcasted_iota(jnp.int32, (Q_PER_KV * nq, 1), 0)
    for g in range(N_KV_HEADS):
        vals = jnp.concatenate([vs_ref[g, kb0:kb0 + nband, :], mv_ref[g], pad], axis=0)
        s = jnp.where(valid, scores[g], NEG_BIG)
        sink = jnp.where(rid < nq, sinks_ref[g, 0],
                         jnp.where(rid < 2 * nq, sinks_ref[g, 1],
                                   jnp.where(rid < 3 * nq, sinks_ref[g, 2], sinks_ref[g, 3])))
        m = jnp.maximum(jnp.max(s, axis=-1, keepdims=True), sink)
        p = jnp.exp(s - m)
        den = jnp.sum(p, axis=-1, keepdims=True) + jnp.exp(sink - m)
        pv = _dot(p.astype(BF16), vals) / den
        for j in range(2):
            c0 = (2 * g + j) * LANES
            oatt_ref[r0:r0 + nq, c0:c0 + LANES] = jnp.where(
                lo, pv[(2 * j) * nq:(2 * j + 1) * nq], pv[(2 * j + 1) * nq:(2 * j + 2) * nq])
    yield


def _ffn_steps(acc, hn2, w_fi_ref, w_fo_ref, bounds, out):
    for c0, c1 in bounds:
        gate = _dot(hn2, w_fi_ref[:, c0:c1])
        yield
        up = _dot(hn2, w_fi_ref[:, D_FF + c0:D_FF + c1])
        yield
        acc = acc + _dot((_silu(gate) * up).astype(BF16), w_fo_ref[c0:c1, :])
        yield
    out.append(acc)


def _run(*gens):
    for g in gens:
        for _ in g:
            pass


def _mix_out(x, ohg_ref, g_ref, oatt_ref, hgg, ang, w_out_ref):
    parts = []
    for h in range(HG_HEADS):
        ls = slice(h * HG_DIM, (h + 1) * HG_DIM)
        o = ohg_ref[:, ls]
        on = o * lax.rsqrt(jnp.mean(o * o, axis=-1, keepdims=True) + EPS)
        parts.append(on * hgg[:, ls] * _silu(g_ref[:, ls]))
    parts.append(_rms(oatt_ref[...], ang))
    mixed = _dot(jnp.concatenate(parts, axis=-1).astype(BF16), w_out_ref[...])
    return x + mixed


def _ffn_bounds(parts):
    tiles = D_FF // MXU_TILE
    cuts = [MXU_TILE * ((tiles * p) // parts) for p in range(parts + 1)]
    return list(zip(cuts[:-1], cuts[1:]))


def _store_window(ak, av, ks_ref, vs_ref, t):
    k0, k1 = _dup_heads(ak)
    v0, v1 = _dup_heads(av)
    ks_ref[0, WINDOW:WINDOW + t, :] = k0.astype(BF16)
    ks_ref[1, WINDOW:WINDOW + t, :] = k1.astype(BF16)
    vs_ref[0, WINDOW:WINDOW + t, :] = v0.astype(BF16)
    vs_ref[1, WINDOW:WINDOW + t, :] = v1.astype(BF16)


def _meta_kernel(x_ref, tab_ref, lbp_ref, g1_ref, w_in_ref,
                 mak_ref, mav_ref, mk2_ref, mv2_ref, st_ref):
    hn = _rms(x_ref[...], g1_ref[...]).astype(BF16)
    col = lambda c0, w: _dot(hn, w_in_ref[:, c0:c0 + w])
    lb = _forget_lower_bound(lbp_ref)
    f = lb + (1.0 - lb) * _sigmoid(col(COL_F, HG_WIDTH))
    b = _cumsum_blocks(jnp.log2(f), N_META)
    k_end = ((1.0 - f) * jnp.exp2(b[N_META - 1:N_META] - b)).astype(BF16)
    v16 = col(COL_I, HG_WIDTH).astype(BF16)
    for h in range(HG_HEADS):
        ls = slice(h * HG_DIM, (h + 1) * HG_DIM)
        st_ref[h] = lax.dot_general(v16[:, ls], k_end[:, ls], TN_DIMS, preferred_element_type=F32)
    ak = _rope(col(COL_AK, KV_WIDTH), tab_ref[...])
    av = col(COL_AV, KV_WIDTH)
    mak_ref[...] = ak
    mav_ref[...] = av
    k0, k1 = _dup_heads(ak)
    v0, v1 = _dup_heads(av)
    mk2_ref[0], mk2_ref[1] = k0.astype(BF16), k1.astype(BF16)
    mv2_ref[0], mv2_ref[1] = v0.astype(BF16), v1.astype(BF16)


def _prompt_kernel(x_ref, tab_ref, stm_ref, mk2_ref, mv2_ref, sinks_ref, lbp_ref, g1_ref, hgg_ref, ang_ref,
                   n2_ref, fn_ref, w_in_ref, w_out_ref, w_fi_ref, w_fo_ref,
                   y_ref, wk_ref, wv_ref, sto_ref,
                   q_s, k_s, v_s, b_s, aq_s, st_s, ks_s, vs_s, x2_s, g2_s, ohg2_s, oatt2_s, acc_s, hn2_s,
                   *, tile, tiles_per_stream, n_tiles):
    s = pl.program_id(0)
    cur = lax.rem(s, 2)
    prev = 1 - cur
    j = lax.rem(jnp.minimum(s, n_tiles - 1), tiles_per_stream)

    @pl.when(s == 0)
    def _():
        for ref in (g2_s, ohg2_s, oatt2_s):
            ref[1] = jnp.zeros((tile, HG_WIDTH), F32)
        x2_s[1] = jnp.zeros((tile, D_MODEL), F32)

    @pl.when(j == 0)
    def _():
        st_s[...] = stm_ref[...]
        ks_s[:, 0:WINDOW, :] = jnp.zeros((N_KV_HEADS, WINDOW, LANES), BF16)
        vs_s[:, 0:WINDOW, :] = jnp.zeros((N_KV_HEADS, WINDOW, LANES), BF16)

    once = jnp.minimum(s, 0) + 1
    n_chunks = tile // CHUNK

    def project_piece():
        x = x_ref[0]
        x2_s[cur] = x
        kv = []
        proj = _project_steps(x, tab_ref[...], _forget_lower_bound(lbp_ref), g1_ref[...], w_in_ref,
                              q_s, k_s, v_s, b_s, g2_s.at[cur], aq_s, CHUNK, kv)
        next(proj)
        h1 = _mix_out(x2_s[prev], ohg2_s.at[prev], g2_s.at[prev], oatt2_s.at[prev], hgg_ref[...], ang_ref[...],
                      w_out_ref)
        acc_s[...] = h1
        hn2_s[...] = _rms(h1, n2_ref[...]).astype(BF16)
        _run(proj)
        ak, av = kv
        _store_window(ak, av, ks_s, vs_s, tile)
        wk_ref[0] = ak[tile - WINDOW:tile]
        wv_ref[0] = av[tile - WINDOW:tile]

    def chunk_piece(c, bounds):
        r0 = c * CHUNK
        first_valid = jnp.maximum(WINDOW - (j * n_chunks + c) * CHUNK, 0)
        ffn_out = []
        ffn = _ffn_steps(acc_s[...], hn2_s[...], w_fi_ref, w_fo_ref, [bounds], ffn_out)
        mix = _mixer_steps(q_s, k_s, v_s, b_s, st_s, ohg2_s.at[cur], aq_s, ks_s, vs_s, mk2_ref, mv2_ref,
                           sinks_ref, oatt2_s.at[cur], r0, CHUNK, r0, WINDOW + CHUNK, first_valid)
        for gen in (ffn, mix, mix, ffn, mix, ffn):
            next(gen)
        _run(ffn, mix)
        acc_s[...] = ffn_out[0]

    def final_piece():
        y_ref[0] = _rms(acc_s[...], fn_ref[...])
        ks_s[:, 0:WINDOW, :] = ks_s[:, tile:tile + WINDOW, :]
        vs_s[:, 0:WINDOW, :] = vs_s[:, tile:tile + WINDOW, :]

    pieces = [project_piece]
    pieces += [functools.partial(chunk_piece, c, bounds) for c, bounds in enumerate(_ffn_bounds(n_chunks))]
    pieces += [final_piece]
    assert sum(REGION_GROUPS) == len(pieces)
    start = 0
    for size in REGION_GROUPS:
        group = pieces[start:start + size]
        start += size
        lax.fori_loop(0, once, lambda _, carry, group=group: ([p() for p in group], carry)[1], 0)

    @pl.when(jnp.logical_and(j == tiles_per_stream - 1, s < n_tiles))
    def _():
        for h in range(HG_HEADS):
            sto_ref[0, h] = st_s[h].T


def _sample_kernel(x_ref, tab_ref, st0_ref, cmk_ref, cmv_ref, cwk_ref, cwv_ref, sinks_ref, lbp_ref, g1_ref,
                   hgg_ref, ang_ref, n2_ref, fn_ref, w_in_ref, w_out_ref, w_fi_ref, w_fo_ref,
                   y_ref, nk_ref, nv_ref, sto_ref,
                   q_s, k_s, v_s, b_s, g_s, aq_s, ohg_s, oatt_s, st_s, ks_s, vs_s, mk_s, mv_s, *, streams, t):
    for b in range(streams):
        for h in range(HG_HEADS):
            st_s[b, h] = st0_ref[b, h].T
        for src, dst in ((cmk_ref, mk_s), (cmv_ref, mv_s)):
            d0, d1 = _dup_heads(src[b])
            dst[b, 0], dst[b, 1] = d0.astype(BF16), d1.astype(BF16)
        for src, dst in ((cwk_ref, ks_s), (cwv_ref, vs_s)):
            d0, d1 = _dup_heads(src[b])
            dst[b, 0, 0:WINDOW, :], dst[b, 1, 0:WINDOW, :] = d0.astype(BF16), d1.astype(BF16)

    x = x_ref[...]
    kv = []
    _run(_project_steps(x, tab_ref[...], _forget_lower_bound(lbp_ref), g1_ref[...], w_in_ref,
                        q_s, k_s, v_s, b_s, g_s, aq_s, t, kv))
    ak, av = kv
    nk_ref[...] = ak
    nv_ref[...] = av
    for b in range(streams):
        rs = slice(b * t, (b + 1) * t)
        _store_window(ak[rs], av[rs], ks_s.at[b], vs_s.at[b], t)
        _run(_mixer_steps(q_s, k_s, v_s, b_s, st_s.at[b], ohg_s, aq_s, ks_s.at[b], vs_s.at[b], mk_s.at[b], mv_s.at[b],
                          sinks_ref, oatt_s, b * t, t, 0, WINDOW + t, 0))
    h1 = _mix_out(x, ohg_s, g_s, oatt_s, hgg_ref[...], ang_ref[...], w_out_ref)
    hn2 = _rms(h1, n2_ref[...]).astype(BF16)
    ffn_out = []
    _run(_ffn_steps(h1, hn2, w_fi_ref, w_fo_ref, [(0, D_FF)], ffn_out))
    y_ref[...] = _rms(ffn_out[0], fn_ref[...])
    for b in range(streams):
        for h in range(HG_HEADS):
            sto_ref[b, h] = st_s[b, h].T


def _rope_table(pos):
    inv = ROPE_THETA ** (-jnp.arange(0, ATT_HEAD_DIM, 2, dtype=F32) / ATT_HEAD_DIM)
    ang = pos.astype(F32)[:, None] * inv[None, :]
    ang = jnp.concatenate([ang, ang, ang, ang], axis=-1)
    first = (jnp.arange(LANES) % ATT_HEAD_DIM) < ATT_HEAD_DIM // 2
    cos, sin = jnp.cos(ang), jnp.sin(ang)
    return jnp.concatenate([cos, jnp.where(first, -sin, 0.0), jnp.where(first, 0.0, sin)], axis=-1)


def _vmem():
    return pl.BlockSpec(memory_space=pltpu.VMEM)


def _sample_scratch(streams, t):
    wide = lambda: pltpu.VMEM((streams * t, HG_WIDTH), F32)
    window = lambda: pltpu.VMEM((streams, N_KV_HEADS, WINDOW + t, LANES), BF16)
    meta = lambda: pltpu.VMEM((streams, N_KV_HEADS, N_META, LANES), BF16)
    return [wide() for _ in range(8)] + [pltpu.VMEM((streams, HG_HEADS, HG_DIM, HG_DIM), F32),
                                         window(), window(), meta(), meta()]


def _prompt_tile(seq):
    for t in (256, 128):
        if seq % t == 0:
            return t
    raise ValueError(f"sequence length {seq} must be a multiple of 128")


def kernel(x_prompt, x_sample, cache_meta_k, cache_meta_v, cache_win_k, cache_win_v, state_hgrn, meta_tokens,
           norm1, w_in, lb_param, hg_norm, attn_sinks, attn_norm, w_out, norm2, w_ffn_in, w_ffn_out, final_norm):
    B, S, D = x_prompt.shape
    Bd, T, _ = x_sample.shape
    assert norm1.shape[0] == 1 and lb_param.shape[0] == 2, "one-layer model"
    assert D == D_MODEL and T % SUB_BLOCK == 0 and T <= CHUNK and cache_win_k.shape[2] == WINDOW
    tile = _prompt_tile(S)

    w_in_b, w_out_b = w_in[0].astype(BF16), w_out[0].astype(BF16)
    w_fi_b, w_fo_b = w_ffn_in[0].astype(BF16), w_ffn_out[0].astype(BF16)
    row = lambda a: a.reshape(1, -1).astype(F32)
    g1, hgg, ang, n2, fn = row(norm1[0]), row(hg_norm[0]), row(attn_norm[0]), row(norm2[0]), row(final_norm)
    lbp = lb_param.astype(F32)
    sinks = attn_sinks[0].astype(F32)
    tab_m = _rope_table(jnp.arange(N_META))
    tab_p = _rope_table(N_META + jnp.arange(S))
    tab_s = _rope_table(N_META + PAST_LEN + jnp.arange(T))
    params = pltpu.CompilerParams(vmem_limit_bytes=VMEM_LIMIT)

    mak, mav, mk2, mv2, stm = pl.pallas_call(
        _meta_kernel,
        out_shape=(jax.ShapeDtypeStruct((N_META, KV_WIDTH), F32), jax.ShapeDtypeStruct((N_META, KV_WIDTH), F32),
                   jax.ShapeDtypeStruct((N_KV_HEADS, N_META, LANES), BF16),
                   jax.ShapeDtypeStruct((N_KV_HEADS, N_META, LANES), BF16),
                   jax.ShapeDtypeStruct((HG_HEADS, HG_DIM, HG_DIM), F32)),
        in_specs=[_vmem()] * 5, out_specs=tuple(_vmem() for _ in range(5)),
        compiler_params=params, name="meta",
    )(meta_tokens.astype(F32), tab_m, lbp, g1, w_in_b)

    smem = pl.BlockSpec(memory_space=pltpu.SMEM)
    weights_specs = [_vmem()] * 4
    nj = S // tile
    n_tiles = B * nj
    mix_tile = lambda s: jnp.minimum(s, n_tiles - 1)
    out_tile = lambda s: jnp.maximum(s - 1, 0)
    wide = lambda: pltpu.VMEM((tile, HG_WIDTH), F32)
    slots = lambda w: pltpu.VMEM((2, tile, w), F32)
    y_p, wk, wv, st_p = pl.pallas_call(
        functools.partial(_prompt_kernel, tile=tile, tiles_per_stream=nj, n_tiles=n_tiles),
        grid=(n_tiles + 1,),
        out_shape=(jax.ShapeDtypeStruct((B, S, D), F32), jax.ShapeDtypeStruct((B, WINDOW, KV_WIDTH), F32),
                   jax.ShapeDtypeStruct((B, WINDOW, KV_WIDTH), F32),
                   jax.ShapeDtypeStruct((B, HG_HEADS, HG_DIM, HG_DIM), F32)),
        in_specs=[pl.BlockSpec((1, tile, D), lambda s: (mix_tile(s) // nj, mix_tile(s) % nj, 0)),
                  pl.BlockSpec((tile, 3 * LANES), lambda s: (mix_tile(s) % nj, 0)),
                  _vmem(), _vmem(), _vmem(), smem] + [_vmem()] * 6 + weights_specs,
        out_specs=(pl.BlockSpec((1, tile, D), lambda s: (out_tile(s) // nj, out_tile(s) % nj, 0)),
                   pl.BlockSpec((1, WINDOW, KV_WIDTH), lambda s: (mix_tile(s) // nj, 0, 0)),
                   pl.BlockSpec((1, WINDOW, KV_WIDTH), lambda s: (mix_tile(s) // nj, 0, 0)),
                   pl.BlockSpec((1, HG_HEADS, HG_DIM, HG_DIM), lambda s: (mix_tile(s) // nj, 0, 0, 0))),
        scratch_shapes=[wide() for _ in range(5)] + [
            pltpu.VMEM((HG_HEADS, HG_DIM, HG_DIM), F32),
            pltpu.VMEM((N_KV_HEADS, WINDOW + tile, LANES), BF16),
            pltpu.VMEM((N_KV_HEADS, WINDOW + tile, LANES), BF16),
            slots(D_MODEL), slots(HG_WIDTH), slots(HG_WIDTH), slots(HG_WIDTH),
            pltpu.VMEM((tile, D_MODEL), F32), pltpu.VMEM((tile, D_MODEL), BF16)],
        compiler_params=pltpu.CompilerParams(vmem_limit_bytes=VMEM_LIMIT, dimension_semantics=("arbitrary",)),
        name="prompt",
    )(x_prompt, tab_p, stm, mk2, mv2, sinks, lbp, g1, hgg, ang, n2, fn, w_in_b, w_out_b, w_fi_b, w_fo_b)

    rows = Bd * T
    y_s, nk, nv, st_s = pl.pallas_call(
        functools.partial(_sample_kernel, streams=Bd, t=T),
        out_shape=(jax.ShapeDtypeStruct((rows, D), F32), jax.ShapeDtypeStruct((rows, KV_WIDTH), F32),
                   jax.ShapeDtypeStruct((rows, KV_WIDTH), F32),
                   jax.ShapeDtypeStruct((Bd, HG_HEADS, HG_DIM, HG_DIM), F32)),
        in_specs=[_vmem()] * 7 + [smem] + [_vmem()] * 6 + weights_specs,
        out_specs=tuple(_vmem() for _ in range(4)),
        scratch_shapes=_sample_scratch(Bd, T),
        compiler_params=params, name="sample",
    )(x_sample.reshape(rows, D), jnp.tile(tab_s, (Bd, 1)), state_hgrn[0].astype(F32),
      cache_meta_k[0].reshape(Bd, N_META, KV_WIDTH), cache_meta_v[0].reshape(Bd, N_META, KV_WIDTH),
      cache_win_k[0].reshape(Bd, WINDOW, KV_WIDTH), cache_win_v[0].reshape(Bd, WINDOW, KV_WIDTH),
      sinks, lbp, g1, hgg, ang, n2, fn, w_in_b, w_out_b, w_fi_b, w_fo_b)
    y_s = y_s.reshape(Bd, T, D)

    kv5 = lambda a, n, t: a.reshape(1, n, t, N_KV_HEADS, ATT_HEAD_DIM)
    bmeta = lambda a: jnp.broadcast_to(a.reshape(1, 1, N_META, N_KV_HEADS, ATT_HEAD_DIM),
                                       (1, B, N_META, N_KV_HEADS, ATT_HEAD_DIM))
    return (y_p, y_s, bmeta(mak), bmeta(mav), kv5(wk, B, WINDOW), kv5(wv, B, WINDOW), st_p[None],
            kv5(nk, Bd, T), kv5(nv, Bd, T), st_s[None])
```

```python
import functools

import jax
import jax.numpy as jnp
from jax import lax
from jax.experimental import pallas as pl
from jax.experimental.pallas import tpu as pltpu

F32 = jnp.float32
BF16 = jnp.bfloat16

D_MODEL = 1024
CHUNK = 64
N_META = 16
HG_WIDTH = 512
HG_HEADS = 4
HG_DIM = 128
ATT_WIDTH = 512
ATT_HEAD_DIM = 64
N_KV_HEADS = 2
Q_PER_KV = 4
KV_WIDTH = 128
WINDOW = 128
PAST_LEN = 1024
ROPE_THETA = 10000.0
D_FF = 2816
EPS = 1e-6

COL_Q, COL_F, COL_I, COL_G = 0, HG_WIDTH, 2 * HG_WIDTH, 3 * HG_WIDTH
COL_AQ = 4 * HG_WIDTH
COL_AK = COL_AQ + ATT_WIDTH
COL_AV = COL_AK + KV_WIDTH

LANES = 128
SUB_BLOCK = 16
MXU_TILE = 256
KEY_PAD = MXU_TILE
NEG_BIG = -1e30
VMEM_LIMIT = 56 * 1024 * 1024
REGION_GROUPS = (3, 3)

NT_DIMS = (((1,), (1,)), ((), ()))
TN_DIMS = (((0,), (0,)), ((), ()))


def _rms(x, g):
    return x * lax.rsqrt(jnp.mean(x * x, axis=-1, keepdims=True) + EPS) * g


def _sigmoid(x):
    return 1.0 / (1.0 + jnp.exp(-x))


def _silu(x):
    h = 0.5 * x
    return h + h * jnp.tanh(h)


def _dot(a, b):
    return jnp.dot(a, b, preferred_element_type=F32)


def _forget_lower_bound(lbp_ref):
    p = lbp_ref[...]
    m = jnp.max(p, axis=0, keepdims=True)
    e = jnp.exp(p - m)
    return e[0:1] / jnp.sum(e, axis=0, keepdims=True)


def _cumsum_blocks(x, block):
    t = x.shape[0]
    r = lax.broadcasted_iota(jnp.int32, (t, t), 0)
    c = lax.broadcasted_iota(jnp.int32, (t, t), 1)
    d = r - c
    tri = jnp.where(d >= 0, jnp.where(d <= (r & (block - 1)), 1.0, 0.0), 0.0).astype(BF16)
    a1 = x.astype(BF16)
    r1 = x - a1.astype(F32)
    a2 = r1.astype(BF16)
    a3 = (r1 - a2.astype(F32)).astype(BF16)
    return _dot(tri, a1) + _dot(tri, a2) + _dot(tri, a3)


def _rope(x, tab):
    cos, sin_a, sin_b = tab[:, 0:LANES], tab[:, LANES:2 * LANES], tab[:, 2 * LANES:3 * LANES]
    return x * cos + pltpu.roll(x, 96, 1) * sin_a + pltpu.roll(x, 32, 1) * sin_b


def _dup_heads(x):
    lo = lax.broadcasted_iota(jnp.int32, (1, LANES), 1) < ATT_HEAD_DIM
    sw = pltpu.roll(x, ATT_HEAD_DIM, 1)
    return jnp.where(lo, x, sw), jnp.where(lo, sw, x)


def _project_steps(x, tab, lb, g1, w_in_ref, q_ref, k_ref, v_ref, b_ref, g_ref, aq_ref, block, out):
    hn = _rms(x, g1).astype(BF16)
    col = lambda c0, w: _dot(hn, w_in_ref[:, c0:c0 + w])
    f = lb + (1.0 - lb) * _sigmoid(col(COL_F, HG_WIDTH))
    k_ref[...] = 1.0 - f
    b_ref[...] = jnp.log2(f)
    q_ref[...] = _silu(col(COL_Q, HG_WIDTH))
    g_ref[...] = col(COL_G, HG_WIDTH)
    v_ref[...] = col(COL_I, HG_WIDTH)
    aq = col(COL_AQ, ATT_WIDTH)
    for j in range(ATT_WIDTH // LANES):
        ls = slice(j * LANES, (j + 1) * LANES)
        aq_ref[:, ls] = _rope(aq[:, ls], tab) * (ATT_HEAD_DIM ** -0.5)
    akv = col(COL_AK, 2 * KV_WIDTH)
    out += [_rope(akv[:, 0:KV_WIDTH], tab), akv[:, KV_WIDTH:2 * KV_WIDTH]]
    yield
    b_ref[...] = _cumsum_blocks(b_ref[...], block)


def _mixer_steps(q_ref, k_ref, v_ref, b_ref, st_ref, ohg_ref, aq_ref, ks_ref, vs_ref, mk_ref, mv_ref, sinks_ref,
                 oatt_ref, r0, length, kb0, nband, first_valid):
    nb = length // SUB_BLOCK
    nq = length
    row8 = lax.broadcasted_iota(jnp.int32, (8, 1), 0)
    rowl = lax.broadcasted_iota(jnp.int32, (length, 1), 0)
    lo = lax.broadcasted_iota(jnp.int32, (1, LANES), 1) < ATT_HEAD_DIM
    heads = [slice(h * HG_DIM, (h + 1) * HG_DIM) for h in range(HG_HEADS)]
    rows = lambda ref, ls, a, n: ref[r0 + a:r0 + a + n, ls]

    o_inter, st_inc, a_off, scores = [], [], [], []
    for ls, h in zip(heads, range(HG_HEADS)):
        q, k, v, b = (rows(r, ls, 0, length) for r in (q_ref, k_ref, v_ref, b_ref))
        b_last = rows(b_ref, ls, length - 1, 1)
        o_inter.append(lax.dot_general((q * jnp.exp2(b)).astype(BF16), st_ref[h].astype(BF16), NT_DIMS,
                                       preferred_element_type=F32))
        k_end = (k * jnp.exp2(b_last - b)).astype(BF16)
        st_inc.append(lax.dot_general(v.astype(BF16), k_end, TN_DIMS, preferred_element_type=F32))
        blocks = [jnp.zeros((SUB_BLOCK, length), F32)]
        for i in range(1, nb):
            c_i = rows(b_ref, ls, SUB_BLOCK * i - 1, 1)
            rs = slice(SUB_BLOCK * i, SUB_BLOCK * (i + 1))
            q_i = (q[rs] * jnp.exp2(b[rs] - c_i)).astype(BF16)
            k_i = jnp.where(rowl < SUB_BLOCK * i, k * jnp.exp2(jnp.minimum(c_i - b, 0.0)), 0.0).astype(BF16)
            blocks.append(lax.dot_general(q_i, k_i, NT_DIMS, preferred_element_type=F32))
        a_off.append(blocks)
    pad = jnp.zeros((KEY_PAD - nband - N_META, LANES), BF16)
    for g in range(N_KV_HEADS):
        keys = jnp.concatenate([ks_ref[g, kb0:kb0 + nband, :], mk_ref[g], pad], axis=0)
        qs = []
        for j in range(2):
            c0 = (2 * g + j) * LANES
            qj = aq_ref[r0:r0 + nq, c0:c0 + LANES]
            qs += [jnp.where(lo, qj, 0.0), jnp.where(lo, 0.0, qj)]
        qst = jnp.concatenate(qs, axis=0).astype(BF16)
        scores.append(lax.dot_general(qst, keys, NT_DIMS, preferred_element_type=F32))
    yield

    lane = lax.broadcasted_iota(jnp.int32, (1, LANES), 1)
    a_diag = []
    for ls in heads:
        groups = []
        for g8 in range(length // 8):
            t0 = 8 * g8
            blk0 = (t0 // SUB_BLOCK) * SUB_BLOCK
            q8, b8 = rows(q_ref, ls, t0, 8), rows(b_ref, ls, t0, 8)
            acc = jnp.zeros((8, LANES), F32)
            for s in range(blk0, t0 + 8):
                b_s, k_s = rows(b_ref, ls, s, 1), rows(k_ref, ls, s, 1)
                w = jnp.sum(q8 * k_s * jnp.exp2(b8 - b_s), axis=-1, keepdims=True)
                acc = jnp.where(lane == s, w, acc)
            groups.append(jnp.where(lane <= t0 + row8, acc, 0.0)[:, 0:length])
        a_diag.append(jnp.concatenate(groups, axis=0))
    yield

    for ls, h in zip(heads, range(HG_HEADS)):
        b_last = rows(b_ref, ls, length - 1, 1)
        st_ref[h] = st_ref[h] * jnp.exp2(b_last) + st_inc[h]
        a = (jnp.concatenate(a_off[h], axis=0) + a_diag[h]).astype(BF16)
        ohg_ref[r0:r0 + length, ls] = o_inter[h] + _dot(a, rows(v_ref, ls, 0, length).astype(BF16))
    kidx = lax.broadcasted_iota(jnp.int32, (1, KEY_PAD), 1)
    valid = jnp.logical_and(kidx >= first_valid, kidx < nband + N_META)
    rid = lax.broadcasted_iota(jnp.int32, (Q_PER_KV * nq, 1), 0)
    for g in range(N_KV_HEADS):
        vals = jnp.concatenate([vs_ref[g, kb0:kb0 + nband, :], mv_ref[g], pad], axis=0)
        s = jnp.where(valid, scores[g], NEG_BIG)
        sink = jnp.where(rid < nq, sinks_ref[g, 0],
                         jnp.where(rid < 2 * nq, sinks_ref[g, 1],
                                   jnp.where(rid < 3 * nq, sinks_ref[g, 2], sinks_ref[g, 3])))
        m = jnp.maximum(jnp.max(s, axis=-1, keepdims=True), sink)
        p = jnp.exp(s - m)
        den = jnp.sum(p, axis=-1, keepdims=True) + jnp.exp(sink - m)
        pv = _dot(p.astype(BF16), vals) / den
        for j in range(2):
            c0 = (2 * g + j) * LANES
            oatt_ref[r0:r0 + nq, c0:c0 + LANES] = jnp.where(
                lo, pv[(2 * j) * nq:(2 * j + 1) * nq], pv[(2 * j + 1) * nq:(2 * j + 2) * nq])
    yield


def _ffn_steps(acc, hn2, w_fi_ref, w_fo_ref, bounds, out):
    for c0, c1 in bounds:
        gate = _dot(hn2, w_fi_ref[:, c0:c1])
        yield
        up = _dot(hn2, w_fi_ref[:, D_FF + c0:D_FF + c1])
        yield
        acc = acc + _dot((_silu(gate) * up).astype(BF16), w_fo_ref[c0:c1, :])
        yield
    out.append(acc)


def _run(*gens):
    for g in gens:
        for _ in g:
            pass


def _mix_out(x, ohg_ref, g_ref, oatt_ref, hgg, ang, w_out_ref):
    parts = []
    for h in range(HG_HEADS):
        ls = slice(h * HG_DIM, (h + 1) * HG_DIM)
        o = ohg_ref[:, ls]
        on = o * lax.rsqrt(jnp.mean(o * o, axis=-1, keepdims=True) + EPS)
        parts.append(on * hgg[:, ls] * _silu(g_ref[:, ls]))
    parts.append(_rms(oatt_ref[...], ang))
    mixed = _dot(jnp.concatenate(parts, axis=-1).astype(BF16), w_out_ref[...])
    return x + mixed


def _ffn_bounds(parts):
    tiles = D_FF // MXU_TILE
    cuts = [MXU_TILE * ((tiles * p) // parts) for p in range(parts + 1)]
    return list(zip(cuts[:-1], cuts[1:]))


def _store_window(ak, av, ks_ref, vs_ref, t):
    k0, k1 = _dup_heads(ak)
    v0, v1 = _dup_heads(av)
    ks_ref[0, WINDOW:WINDOW + t, :] = k0.astype(BF16)
    ks_ref[1, WINDOW:WINDOW + t, :] = k1.astype(BF16)
    vs_ref[0, WINDOW:WINDOW + t, :] = v0.astype(BF16)
    vs_ref[1, WINDOW:WINDOW + t, :] = v1.astype(BF16)


def _meta_kernel(x_ref, tab_ref, lbp_ref, g1_ref, w_in_ref,
                 mak_ref, mav_ref, mk2_ref, mv2_ref, st_ref):
    hn = _rms(x_ref[...], g1_ref[...]).astype(BF16)
    col = lambda c0, w: _dot(hn, w_in_ref[:, c0:c0 + w])
    lb = _forget_lower_bound(lbp_ref)
    f = lb + (1.0 - lb) * _sigmoid(col(COL_F, HG_WIDTH))
    b = _cumsum_blocks(jnp.log2(f), N_META)
    k_end = ((1.0 - f) * jnp.exp2(b[N_META - 1:N_META] - b)).astype(BF16)
    v16 = col(COL_I, HG_WIDTH).astype(BF16)
    for h in range(HG_HEADS):
        ls = slice(h * HG_DIM, (h + 1) * HG_DIM)
        st_ref[h] = lax.dot_general(v16[:, ls], k_end[:, ls], TN_DIMS, preferred_element_type=F32)
    ak = _rope(col(COL_AK, KV_WIDTH), tab_ref[...])
    av = col(COL_AV, KV_WIDTH)
    mak_ref[...] = ak
    mav_ref[...] = av
    k0, k1 = _dup_heads(ak)
    v0, v1 = _dup_heads(av)
    mk2_ref[0], mk2_ref[1] = k0.astype(BF16), k1.astype(BF16)
    mv2_ref[0], mv2_ref[1] = v0.astype(BF16), v1.astype(BF16)


def _prompt_kernel(x_ref, tab_ref, stm_ref, mk2_ref, mv2_ref, sinks_ref, lbp_ref, g1_ref, hgg_ref, ang_ref,
                   n2_ref, fn_ref, w_in_ref, w_out_ref, w_fi_ref, w_fo_ref,
                   y_ref, wk_ref, wv_ref, sto_ref,
                   q_s, k_s, v_s, b_s, aq_s, st_s, ks_s, vs_s, x2_s, g2_s, ohg2_s, oatt2_s, acc_s, hn2_s,
                   *, tile, tiles_per_stream, n_tiles):
    s = pl.program_id(0)
    cur = lax.rem(s, 2)
    prev = 1 - cur
    j = lax.rem(jnp.minimum(s, n_tiles - 1), tiles_per_stream)

    @pl.when(s == 0)
    def _():
        for ref in (g2_s, ohg2_s, oatt2_s):
            ref[1] = jnp.zeros((tile, HG_WIDTH), F32)
        x2_s[1] = jnp.zeros((tile, D_MODEL), F32)

    @pl.when(j == 0)
    def _():
        st_s[...] = stm_ref[...]
        ks_s[:, 0:WINDOW, :] = jnp.zeros((N_KV_HEADS, WINDOW, LANES), BF16)
        vs_s[:, 0:WINDOW, :] = jnp.zeros((N_KV_HEADS, WINDOW, LANES), BF16)

    once = jnp.minimum(s, 0) + 1
    n_chunks = tile // CHUNK

    def project_piece():
        x = x_ref[0]
        x2_s[cur] = x
        kv = []
        proj = _project_steps(x, tab_ref[...], _forget_lower_bound(lbp_ref), g1_ref[...], w_in_ref,
                              q_s, k_s, v_s, b_s, g2_s.at[cur], aq_s, CHUNK, kv)
        next(proj)
        h1 = _mix_out(x2_s[prev], ohg2_s.at[prev], g2_s.at[prev], oatt2_s.at[prev], hgg_ref[...], ang_ref[...],
                      w_out_ref)
        acc_s[...] = h1
        hn2_s[...] = _rms(h1, n2_ref[...]).astype(BF16)
        _run(proj)
        ak, av = kv
        _store_window(ak, av, ks_s, vs_s, tile)
        wk_ref[0] = ak[tile - WINDOW:tile]
        wv_ref[0] = av[tile - WINDOW:tile]

    def chunk_piece(c, bounds):
        r0 = c * CHUNK
        first_valid = jnp.maximum(WINDOW - (j * n_chunks + c) * CHUNK, 0)
        ffn_out = []
        ffn = _ffn_steps(acc_s[...], hn2_s[...], w_fi_ref, w_fo_ref, [bounds], ffn_out)
        mix = _mixer_steps(q_s, k_s, v_s, b_s, st_s, ohg2_s.at[cur], aq_s, ks_s, vs_s, mk2_ref, mv2_ref,
                           sinks_ref, oatt2_s.at[cur], r0, CHUNK, r0, WINDOW + CHUNK, first_valid)
        for gen in (ffn, mix, mix, ffn, mix, ffn):
            next(gen)
        _run(ffn, mix)
        acc_s[...] = ffn_out[0]

    def final_piece():
        y_ref[0] = _rms(acc_s[...], fn_ref[...])
        ks_s[:, 0:WINDOW, :] = ks_s[:, tile:tile + WINDOW, :]
        vs_s[:, 0:WINDOW, :] = vs_s[:, tile:tile + WINDOW, :]

    pieces = [project_piece]
    pieces += [functools.partial(chunk_piece, c, bounds) for c, bounds in enumerate(_ffn_bounds(n_chunks))]
    pieces += [final_piece]
    assert sum(REGION_GROUPS) == len(pieces)
    start = 0
    for size in REGION_GROUPS:
        group = pieces[start:start + size]
        start += size
        lax.fori_loop(0, once, lambda _, carry, group=group: ([p() for p in group], carry)[1], 0)

    @pl.when(jnp.logical_and(j == tiles_per_stream - 1, s < n_tiles))
    def _():
        for h in range(HG_HEADS):
            sto_ref[0, h] = st_s[h].T


def _sample_kernel(x_ref, tab_ref, st0_ref, cmk_ref, cmv_ref, cwk_ref, cwv_ref, sinks_ref, lbp_ref, g1_ref,
                   hgg_ref, ang_ref, n2_ref, fn_ref, w_in_ref, w_out_ref, w_fi_ref, w_fo_ref,
                   y_ref, nk_ref, nv_ref, sto_ref,
                   q_s, k_s, v_s, b_s, g_s, aq_s, ohg_s, oatt_s, st_s, ks_s, vs_s, mk_s, mv_s, *, streams, t):
    for b in range(streams):
        for h in range(HG_HEADS):
            st_s[b, h] = st0_ref[b, h].T
        for src, dst in ((cmk_ref, mk_s), (cmv_ref, mv_s)):
            d0, d1 = _dup_heads(src[b])
            dst[b, 0], dst[b, 1] = d0.astype(BF16), d1.astype(BF16)
        for src, dst in ((cwk_ref, ks_s), (cwv_ref, vs_s)):
            d0, d1 = _dup_heads(src[b])
            dst[b, 0, 0:WINDOW, :], dst[b, 1, 0:WINDOW, :] = d0.astype(BF16), d1.astype(BF16)

    x = x_ref[...]
    kv = []
    _run(_project_steps(x, tab_ref[...], _forget_lower_bound(lbp_ref), g1_ref[...], w_in_ref,
                        q_s, k_s, v_s, b_s, g_s, aq_s, t, kv))
    ak, av = kv
    nk_ref[...] = ak
    nv_ref[...] = av
    for b in range(streams):
        rs = slice(b * t, (b + 1) * t)
        _store_window(ak[rs], av[rs], ks_s.at[b], vs_s.at[b], t)
        _run(_mixer_steps(q_s, k_s, v_s, b_s, st_s.at[b], ohg_s, aq_s, ks_s.at[b], vs_s.at[b], mk_s.at[b], mv_s.at[b],
                          sinks_ref, oatt_s, b * t, t, 0, WINDOW + t, 0))
    h1 = _mix_out(x, ohg_s, g_s, oatt_s, hgg_ref[...], ang_ref[...], w_out_ref)
    hn2 = _rms(h1, n2_ref[...]).astype(BF16)
    ffn_out = []
    _run(_ffn_steps(h1, hn2, w_fi_ref, w_fo_ref, [(0, D_FF)], ffn_out))
    y_ref[...] = _rms(ffn_out[0], fn_ref[...])
    for b in range(streams):
        for h in range(HG_HEADS):
            sto_ref[b, h] = st_s[b, h].T


def _rope_table(pos):
    inv = ROPE_THETA ** (-jnp.arange(0, ATT_HEAD_DIM, 2, dtype=F32) / ATT_HEAD_DIM)
    ang = pos.astype(F32)[:, None] * inv[None, :]
    ang = jnp.concatenate([ang, ang, ang, ang], axis=-1)
    first = (jnp.arange(LANES) % ATT_HEAD_DIM) < ATT_HEAD_DIM // 2
    cos, sin = jnp.cos(ang), jnp.sin(ang)
    return jnp.concatenate([cos, jnp.where(first, -sin, 0.0), jnp.where(first, 0.0, sin)], axis=-1)


def _vmem():
    return pl.BlockSpec(memory_space=pltpu.VMEM)


def _sample_scratch(streams, t):
    wide = lambda: pltpu.VMEM((streams * t, HG_WIDTH), F32)
    window = lambda: pltpu.VMEM((streams, N_KV_HEADS, WINDOW + t, LANES), BF16)
    meta = lambda: pltpu.VMEM((streams, N_KV_HEADS, N_META, LANES), BF16)
    return [wide() for _ in range(8)] + [pltpu.VMEM((streams, HG_HEADS, HG_DIM, HG_DIM), F32),
                                         window(), window(), meta(), meta()]


def _prompt_tile(seq):
    for t in (256, 128):
        if seq % t == 0:
            return t
    raise ValueError(f"sequence length {seq} must be a multiple of 128")


def kernel(x_prompt, x_sample, cache_meta_k, cache_meta_v, cache_win_k, cache_win_v, state_hgrn, meta_tokens,
           norm1, w_in, lb_param, hg_norm, attn_sinks, attn_norm, w_out, norm2, w_ffn_in, w_ffn_out, final_norm):
    B, S, D = x_prompt.shape
    Bd, T, _ = x_sample.shape
    assert norm1.shape[0] == 1 and lb_param.shape[0] == 2, "one-layer model"
    assert D == D_MODEL and T % SUB_BLOCK == 0 and T <= CHUNK and cache_win_k.shape[2] == WINDOW
    tile = _prompt_tile(S)

    w_in_b, w_out_b = w_in[0].astype(BF16), w_out[0].astype(BF16)
    w_fi_b, w_fo_b = w_ffn_in[0].astype(BF16), w_ffn_out[0].astype(BF16)
    row = lambda a: a.reshape(1, -1).astype(F32)
    g1, hgg, ang, n2, fn = row(norm1[0]), row(hg_norm[0]), row(attn_norm[0]), row(norm2[0]), row(final_norm)
    lbp = lb_param.astype(F32)
    sinks = attn_sinks[0].astype(F32)
    tab_m = _rope_table(jnp.arange(N_META))
    tab_p = _rope_table(N_META + jnp.arange(S))
    tab_s = _rope_table(N_META + PAST_LEN + jnp.arange(T))
    params = pltpu.CompilerParams(vmem_limit_bytes=VMEM_LIMIT)

    mak, mav, mk2, mv2, stm = pl.pallas_call(
        _meta_kernel,
        out_shape=(jax.ShapeDtypeStruct((N_META, KV_WIDTH), F32), jax.ShapeDtypeStruct((N_META, KV_WIDTH), F32),
                   jax.ShapeDtypeStruct((N_KV_HEADS, N_META, LANES), BF16),
                   jax.ShapeDtypeStruct((N_KV_HEADS, N_META, LANES), BF16),
                   jax.ShapeDtypeStruct((HG_HEADS, HG_DIM, HG_DIM), F32)),
        in_specs=[_vmem()] * 5, out_specs=tuple(_vmem() for _ in range(5)),
        compiler_params=params, name="meta",
    )(meta_tokens.astype(F32), tab_m, lbp, g1, w_in_b)

    smem = pl.BlockSpec(memory_space=pltpu.SMEM)
    weights_specs = [_vmem()] * 4
    nj = S // tile
    n_tiles = B * nj
    mix_tile = lambda s: jnp.minimum(s, n_tiles - 1)
    out_tile = lambda s: jnp.maximum(s - 1, 0)
    wide = lambda: pltpu.VMEM((tile, HG_WIDTH), F32)
    slots = lambda w: pltpu.VMEM((2, tile, w), F32)
    y_p, wk, wv, st_p = pl.pallas_call(
        functools.partial(_prompt_kernel, tile=tile, tiles_per_stream=nj, n_tiles=n_tiles),
        grid=(n_tiles + 1,),
        out_shape=(jax.ShapeDtypeStruct((B, S, D), F32), jax.ShapeDtypeStruct((B, WINDOW, KV_WIDTH), F32),
                   jax.ShapeDtypeStruct((B, WINDOW, KV_WIDTH), F32),
                   jax.ShapeDtypeStruct((B, HG_HEADS, HG_DIM, HG_DIM), F32)),
        in_specs=[pl.BlockSpec((1, tile, D), lambda s: (mix_tile(s) // nj, mix_tile(s) % nj, 0)),
                  pl.BlockSpec((tile, 3 * LANES), lambda s: (mix_tile(s) % nj, 0)),
                  _vmem(), _vmem(), _vmem(), smem] + [_vmem()] * 6 + weights_specs,
        out_specs=(pl.BlockSpec((1, tile, D), lambda s: (out_tile(s) // nj, out_tile(s) % nj, 0)),
                   pl.BlockSpec((1, WINDOW, KV_WIDTH), lambda s: (mix_tile(s) // nj, 0, 0)),
                   pl.BlockSpec((1, WINDOW, KV_WIDTH), lambda s: (mix_tile(s) // nj, 0, 0)),
                   pl.BlockSpec((1, HG_HEADS, HG_DIM, HG_DIM), lambda s: (mix_tile(s) // nj, 0, 0, 0))),
        scratch_shapes=[wide() for _ in range(5)] + [
            pltpu.VMEM((HG_HEADS, HG_DIM, HG_DIM), F32),
            pltpu.VMEM((N_KV_HEADS, WINDOW + tile, LANES), BF16),
            pltpu.VMEM((N_KV_HEADS, WINDOW + tile, LANES), BF16),
            slots(D_MODEL), slots(HG_WIDTH), slots(HG_WIDTH), slots(HG_WIDTH),
            pltpu.VMEM((tile, D_MODEL), F32), pltpu.VMEM((tile, D_MODEL), BF16)],
        compiler_params=pltpu.CompilerParams(vmem_limit_bytes=VMEM_LIMIT, dimension_semantics=("arbitrary",)),
        name="prompt",
    )(x_prompt, tab_p, stm, mk2, mv2, sinks, lbp, g1, hgg, ang, n2, fn, w_in_b, w_out_b, w_fi_b, w_fo_b)

    rows = Bd * T
    y_s, nk, nv, st_s = pl.pallas_call(
        functools.partial(_sample_kernel, streams=Bd, t=T),
        out_shape=(jax.ShapeDtypeStruct((rows, D), F32), jax.ShapeDtypeStruct((rows, KV_WIDTH), F32),
                   jax.ShapeDtypeStruct((rows, KV_WIDTH), F32),
                   jax.ShapeDtypeStruct((Bd, HG_HEADS, HG_DIM, HG_DIM), F32)),
        in_specs=[_vmem()] * 7 + [smem] + [_vmem()] * 6 + weights_specs,
        out_specs=tuple(_vmem() for _ in range(4)),
        scratch_shapes=_sample_scratch(Bd, T),
        compiler_params=params, name="sample",
    )(x_sample.reshape(rows, D), jnp.tile(tab_s, (Bd, 1)), state_hgrn[0].astype(F32),
      cache_meta_k[0].reshape(Bd, N_META, KV_WIDTH), cache_meta_v[0].reshape(Bd, N_META, KV_WIDTH),
      cache_win_k[0].reshape(Bd, WINDOW, KV_WIDTH), cache_win_v[0].reshape(Bd, WINDOW, KV_WIDTH),
      sinks, lbp, g1, hgg, ang, n2, fn, w_in_b, w_out_b, w_fi_b, w_fo_b)
    y_s = y_s.reshape(Bd, T, D)

    kv5 = lambda a, n, t: a.reshape(1, n, t, N_KV_HEADS, ATT_HEAD_DIM)
    bmeta = lambda a: jnp.broadcast_to(a.reshape(1, 1, N_META, N_KV_HEADS, ATT_HEAD_DIM),
                                       (1, B, N_META, N_KV_HEADS, ATT_HEAD_DIM))
    return (y_p, y_s, bmeta(mak), bmeta(mav), kv5(wk, B, WINDOW), kv5(wv, B, WINDOW), st_p[None],
            kv5(nk, Bd, T), kv5(nv, Bd, T), st_s[None])
```

```python
import functools

import jax
import jax.numpy as jnp
from jax import lax
from jax.experimental import pallas as pl
from jax.experimental.pallas import tpu as pltpu

F32 = jnp.float32
BF16 = jnp.bfloat16

D_MODEL = 1024
CHUNK = 64
N_META = 16
HG_WIDTH = 512
HG_HEADS = 4
HG_DIM = 128
ATT_WIDTH = 512
ATT_HEAD_DIM = 64
N_KV_HEADS = 2
Q_PER_KV = 4
KV_WIDTH = 128
WINDOW = 128
PAST_LEN = 1024
ROPE_THETA = 10000.0
D_FF = 2816
EPS = 1e-6

COL_Q, COL_F, COL_I, COL_G = 0, HG_WIDTH, 2 * HG_WIDTH, 3 * HG_WIDTH
COL_AQ = 4 * HG_WIDTH
COL_AK = COL_AQ + ATT_WIDTH
COL_AV = COL_AK + KV_WIDTH

LANES = 128
SUB_BLOCK = 16
MXU_TILE = 256
KEY_PAD = MXU_TILE
NEG_BIG = -1e30
VMEM_LIMIT = 56 * 1024 * 1024
DMA_TILES = 2
REGION_GROUPS = (3, 3)

NT_DIMS = (((1,), (1,)), ((), ()))
TN_DIMS = (((0,), (0,)), ((), ()))


def _rms(x, g):
    return x * lax.rsqrt(jnp.mean(x * x, axis=-1, keepdims=True) + EPS) * g


def _sigmoid(x):
    return 1.0 / (1.0 + jnp.exp(-x))


def _silu(x):
    h = 0.5 * x
    return h + h * jnp.tanh(h)


def _dot(a, b):
    return jnp.dot(a, b, preferred_element_type=F32)


def _forget_lower_bound(lbp_ref):
    p = lbp_ref[...]
    m = jnp.max(p, axis=0, keepdims=True)
    e = jnp.exp(p - m)
    return e[0:1] / jnp.sum(e, axis=0, keepdims=True)


def _cumsum_blocks(x, block):
    t = x.shape[0]
    r = lax.broadcasted_iota(jnp.int32, (t, t), 0)
    c = lax.broadcasted_iota(jnp.int32, (t, t), 1)
    d = r - c
    tri = jnp.where(d >= 0, jnp.where(d <= (r & (block - 1)), 1.0, 0.0), 0.0).astype(BF16)
    a1 = x.astype(BF16)
    r1 = x - a1.astype(F32)
    a2 = r1.astype(BF16)
    a3 = (r1 - a2.astype(F32)).astype(BF16)
    return _dot(tri, a1) + _dot(tri, a2) + _dot(tri, a3)


def _rope(x, tab):
    cos, sin_a, sin_b = tab[:, 0:LANES], tab[:, LANES:2 * LANES], tab[:, 2 * LANES:3 * LANES]
    return x * cos + pltpu.roll(x, 96, 1) * sin_a + pltpu.roll(x, 32, 1) * sin_b


def _dup_heads(x):
    lo = lax.broadcasted_iota(jnp.int32, (1, LANES), 1) < ATT_HEAD_DIM
    sw = pltpu.roll(x, ATT_HEAD_DIM, 1)
    return jnp.where(lo, x, sw), jnp.where(lo, sw, x)


def _project_steps(x, tab, lb, g1, w_in_ref, q_ref, k_ref, v_ref, b_ref, g_ref, aq_ref, block, out):
    hn = _rms(x, g1).astype(BF16)
    col = lambda c0, w: _dot(hn, w_in_ref[:, c0:c0 + w])
    f = lb + (1.0 - lb) * _sigmoid(col(COL_F, HG_WIDTH))
    k_ref[...] = 1.0 - f
    b_ref[...] = jnp.log2(f)
    q_ref[...] = _silu(col(COL_Q, HG_WIDTH))
    g_ref[...] = col(COL_G, HG_WIDTH)
    v_ref[...] = col(COL_I, HG_WIDTH)
    aq = col(COL_AQ, ATT_WIDTH)
    for j in range(ATT_WIDTH // LANES):
        ls = slice(j * LANES, (j + 1) * LANES)
        aq_ref[:, ls] = _rope(aq[:, ls], tab) * (ATT_HEAD_DIM ** -0.5)
    akv = col(COL_AK, 2 * KV_WIDTH)
    out += [_rope(akv[:, 0:KV_WIDTH], tab), akv[:, KV_WIDTH:2 * KV_WIDTH]]
    yield
    b_ref[...] = _cumsum_blocks(b_ref[...], block)


def _mixer_steps(q_ref, k_ref, v_ref, b_ref, st_ref, ohg_ref, aq_ref, ks_ref, vs_ref, mk_ref, mv_ref, sinks_ref,
                 oatt_ref, r0, length, kb0, nband, first_valid):
    nb = length // SUB_BLOCK
    nq = length
    row8 = lax.broadcasted_iota(jnp.int32, (8, 1), 0)
    rowl = lax.broadcasted_iota(jnp.int32, (length, 1), 0)
    lo = lax.broadcasted_iota(jnp.int32, (1, LANES), 1) < ATT_HEAD_DIM
    heads = [slice(h * HG_DIM, (h + 1) * HG_DIM) for h in range(HG_HEADS)]
    rows = lambda ref, ls, a, n: ref[r0 + a:r0 + a + n, ls]

    o_inter, st_inc, a_off, scores = [], [], [], []
    for ls, h in zip(heads, range(HG_HEADS)):
        q, k, v, b = (rows(r, ls, 0, length) for r in (q_ref, k_ref, v_ref, b_ref))
        b_last = rows(b_ref, ls, length - 1, 1)
        o_inter.append(lax.dot_general((q * jnp.exp2(b)).astype(BF16), st_ref[h].astype(BF16), NT_DIMS,
                                       preferred_element_type=F32))
        k_end = (k * jnp.exp2(b_last - b)).astype(BF16)
        st_inc.append(lax.dot_general(v.astype(BF16), k_end, TN_DIMS, preferred_element_type=F32))
        blocks = [jnp.zeros((SUB_BLOCK, length), F32)]
        for i in range(1, nb):
            c_i = rows(b_ref, ls, SUB_BLOCK * i - 1, 1)
            rs = slice(SUB_BLOCK * i, SUB_BLOCK * (i + 1))
            q_i = (q[rs] * jnp.exp2(b[rs] - c_i)).astype(BF16)
            k_i = jnp.where(rowl < SUB_BLOCK * i, k * jnp.exp2(jnp.minimum(c_i - b, 0.0)), 0.0).astype(BF16)
            blocks.append(lax.dot_general(q_i, k_i, NT_DIMS, preferred_element_type=F32))
        a_off.append(blocks)
    pad = jnp.zeros((KEY_PAD - nband - N_META, LANES), BF16)
    for g in range(N_KV_HEADS):
        keys = jnp.concatenate([ks_ref[g, kb0:kb0 + nband, :], mk_ref[g], pad], axis=0)
        qs = []
        for j in range(2):
            c0 = (2 * g + j) * LANES
            qj = aq_ref[r0:r0 + nq, c0:c0 + LANES]
            qs += [jnp.where(lo, qj, 0.0), jnp.where(lo, 0.0, qj)]
        qst = jnp.concatenate(qs, axis=0).astype(BF16)
        scores.append(lax.dot_general(qst, keys, NT_DIMS, preferred_element_type=F32))
    yield

    lane = lax.broadcasted_iota(jnp.int32, (1, LANES), 1)
    a_diag = []
    for ls in heads:
        groups = []
        for g8 in range(length // 8):
            t0 = 8 * g8
            blk0 = (t0 // SUB_BLOCK) * SUB_BLOCK
            q8, b8 = rows(q_ref, ls, t0, 8), rows(b_ref, ls, t0, 8)
            acc = jnp.zeros((8, LANES), F32)
            for s in range(blk0, t0 + 8):
                b_s, k_s = rows(b_ref, ls, s, 1), rows(k_ref, ls, s, 1)
                w = jnp.sum(q8 * k_s * jnp.exp2(b8 - b_s), axis=-1, keepdims=True)
                acc = jnp.where(lane == s, w, acc)
            groups.append(jnp.where(lane <= t0 + row8, acc, 0.0)[:, 0:length])
        a_diag.append(jnp.concatenate(groups, axis=0))
    yield

    for ls, h in zip(heads, range(HG_HEADS)):
        b_last = rows(b_ref, ls, length - 1, 1)
        st_ref[h] = st_ref[h] * jnp.exp2(b_last) + st_inc[h]
        a = (jnp.concatenate(a_off[h], axis=0) + a_diag[h]).astype(BF16)
        ohg_ref[r0:r0 + length, ls] = o_inter[h] + _dot(a, rows(v_ref, ls, 0, length).astype(BF16))
    kidx = lax.broadcasted_iota(jnp.int32, (1, KEY_PAD), 1)
    valid = jnp.logical_and(kidx >= first_valid, kidx < nband + N_META)
    rid = lax.broadcasted_iota(jnp.int32, (Q_PER_KV * nq, 1), 0)
    for g in range(N_KV_HEADS):
        vals = jnp.concatenate([vs_ref[g, kb0:kb0 + nband, :], mv_ref[g], pad], axis=0)
        s = jnp.where(valid, scores[g], NEG_BIG)
        sink = jnp.where(rid < nq, sinks_ref[g, 0],
                         jnp.where(rid < 2 * nq, sinks_ref[g, 1],
                                   jnp.where(rid < 3 * nq, sinks_ref[g, 2], sinks_ref[g, 3])))
        m = jnp.maximum(jnp.max(s, axis=-1, keepdims=True), sink)
        p = jnp.exp(s - m)
        den = jnp.sum(p, axis=-1, keepdims=True) + jnp.exp(sink - m)
        pv = _dot(p.astype(BF16), vals) / den
        for j in range(2):
            c0 = (2 * g + j) * LANES
            oatt_ref[r0:r0 + nq, c0:c0 + LANES] = jnp.where(
                lo, pv[(2 * j) * nq:(2 * j + 1) * nq], pv[(2 * j + 1) * nq:(2 * j + 2) * nq])
    yield


def _ffn_steps(acc, hn2, w_fi_ref, w_fo_ref, bounds, out):
    for c0, c1 in bounds:
        gate = _dot(hn2, w_fi_ref[:, c0:c1])
        yield
        up = _dot(hn2, w_fi_ref[:, D_FF + c0:D_FF + c1])
        yield
        acc = acc + _dot((_silu(gate) * up).astype(BF16), w_fo_ref[c0:c1, :])
        yield
    out.append(acc)


def _run(*gens):
    for g in gens:
        for _ in g:
            pass


def _mix_out(x, ohg_ref, g_ref, oatt_ref, hgg, ang, w_out_ref):
    parts = []
    for h in range(HG_HEADS):
        ls = slice(h * HG_DIM, (h + 1) * HG_DIM)
        o = ohg_ref[:, ls]
        on = o * lax.rsqrt(jnp.mean(o * o, axis=-1, keepdims=True) + EPS)
        parts.append(on * hgg[:, ls] * _silu(g_ref[:, ls]))
    parts.append(_rms(oatt_ref[...], ang))
    mixed = _dot(jnp.concatenate(parts, axis=-1).astype(BF16), w_out_ref[...])
    return x + mixed


def _ffn_bounds(parts):
    tiles = D_FF // MXU_TILE
    cuts = [MXU_TILE * ((tiles * p) // parts) for p in range(parts + 1)]
    return list(zip(cuts[:-1], cuts[1:]))


def _store_window(ak, av, ks_ref, vs_ref, t):
    k0, k1 = _dup_heads(ak)
    v0, v1 = _dup_heads(av)
    ks_ref[0, WINDOW:WINDOW + t, :] = k0.astype(BF16)
    ks_ref[1, WINDOW:WINDOW + t, :] = k1.astype(BF16)
    vs_ref[0, WINDOW:WINDOW + t, :] = v0.astype(BF16)
    vs_ref[1, WINDOW:WINDOW + t, :] = v1.astype(BF16)


def _meta_kernel(x_ref, tab_ref, lbp_ref, g1_ref, w_in_ref,
                 mak_ref, mav_ref, mk2_ref, mv2_ref, st_ref):
    hn = _rms(x_ref[...], g1_ref[...]).astype(BF16)
    col = lambda c0, w: _dot(hn, w_in_ref[:, c0:c0 + w])
    lb = _forget_lower_bound(lbp_ref)
    f = lb + (1.0 - lb) * _sigmoid(col(COL_F, HG_WIDTH))
    b = _cumsum_blocks(jnp.log2(f), N_META)
    k_end = ((1.0 - f) * jnp.exp2(b[N_META - 1:N_META] - b)).astype(BF16)
    v16 = col(COL_I, HG_WIDTH).astype(BF16)
    for h in range(HG_HEADS):
        ls = slice(h * HG_DIM, (h + 1) * HG_DIM)
        st_ref[h] = lax.dot_general(v16[:, ls], k_end[:, ls], TN_DIMS, preferred_element_type=F32)
    ak = _rope(col(COL_AK, KV_WIDTH), tab_ref[...])
    av = col(COL_AV, KV_WIDTH)
    mak_ref[...] = ak
    mav_ref[...] = av
    k0, k1 = _dup_heads(ak)
    v0, v1 = _dup_heads(av)
    mk2_ref[0], mk2_ref[1] = k0.astype(BF16), k1.astype(BF16)
    mv2_ref[0], mv2_ref[1] = v0.astype(BF16), v1.astype(BF16)


def _prompt_kernel(x_ref, tab_ref, stm_ref, mk2_ref, mv2_ref, sinks_ref, lbp_ref, g1_ref, hgg_ref, ang_ref,
                   n2_ref, fn_ref, w_in_ref, w_out_ref, w_fi_ref, w_fo_ref,
                   y_ref, wk_ref, wv_ref, sto_ref,
                   q_s, k_s, v_s, b_s, aq_s, st_s, ks_s, vs_s, x2_s, g2_s, ohg2_s, oatt2_s, acc_s, hn2_s,
                   *, tile, tiles_per_stream, n_tiles):
    s = pl.program_id(0)
    cur = lax.rem(s, 2)
    prev = 1 - cur
    j = lax.rem(jnp.minimum(s, n_tiles - 1), tiles_per_stream)
    in_rows = pl.ds(pl.multiple_of(lax.rem(j, DMA_TILES) * tile, tile), tile)
    out_rows = pl.ds(pl.multiple_of(lax.rem(jnp.maximum(s - 1, 0), DMA_TILES) * tile, tile), tile)

    @pl.when(s == 0)
    def _():
        for ref in (g2_s, ohg2_s, oatt2_s):
            ref[1] = jnp.zeros((tile, HG_WIDTH), F32)
        x2_s[1] = jnp.zeros((tile, D_MODEL), F32)

    @pl.when(j == 0)
    def _():
        st_s[...] = stm_ref[...]
        ks_s[:, 0:WINDOW, :] = jnp.zeros((N_KV_HEADS, WINDOW, LANES), BF16)
        vs_s[:, 0:WINDOW, :] = jnp.zeros((N_KV_HEADS, WINDOW, LANES), BF16)

    once = jnp.minimum(s, 0) + 1
    n_chunks = tile // CHUNK

    def project_piece():
        x = x_ref[0, in_rows, :]
        x2_s[cur] = x
        kv = []
        proj = _project_steps(x, tab_ref[in_rows, :], _forget_lower_bound(lbp_ref), g1_ref[...], w_in_ref,
                              q_s, k_s, v_s, b_s, g2_s.at[cur], aq_s, CHUNK, kv)
        next(proj)
        h1 = _mix_out(x2_s[prev], ohg2_s.at[prev], g2_s.at[prev], oatt2_s.at[prev], hgg_ref[...], ang_ref[...],
                      w_out_ref)
        acc_s[...] = h1
        hn2_s[...] = _rms(h1, n2_ref[...]).astype(BF16)
        _run(proj)
        ak, av = kv
        _store_window(ak, av, ks_s, vs_s, tile)
        wk_ref[0] = ak[tile - WINDOW:tile]
        wv_ref[0] = av[tile - WINDOW:tile]

    def chunk_piece(c, bounds):
        r0 = c * CHUNK
        first_valid = jnp.maximum(WINDOW - (j * n_chunks + c) * CHUNK, 0)
        ffn_out = []
        ffn = _ffn_steps(acc_s[...], hn2_s[...], w_fi_ref, w_fo_ref, [bounds], ffn_out)
        mix = _mixer_steps(q_s, k_s, v_s, b_s, st_s, ohg2_s.at[cur], aq_s, ks_s, vs_s, mk2_ref, mv2_ref,
                           sinks_ref, oatt2_s.at[cur], r0, CHUNK, r0, WINDOW + CHUNK, first_valid)
        for gen in (ffn, mix, mix, ffn, mix, ffn):
            next(gen)
        _run(ffn, mix)
        acc_s[...] = ffn_out[0]

    def final_piece():
        y_ref[0, out_rows, :] = _rms(acc_s[...], fn_ref[...])
        ks_s[:, 0:WINDOW, :] = ks_s[:, tile:tile + WINDOW, :]
        vs_s[:, 0:WINDOW, :] = vs_s[:, tile:tile + WINDOW, :]

    pieces = [project_piece]
    pieces += [functools.partial(chunk_piece, c, bounds) for c, bounds in enumerate(_ffn_bounds(n_chunks))]
    pieces += [final_piece]
    assert sum(REGION_GROUPS) == len(pieces)
    start = 0
    for size in REGION_GROUPS:
        group = pieces[start:start + size]
        start += size
        lax.fori_loop(0, once, lambda _, carry, group=group: ([p() for p in group], carry)[1], 0)

    @pl.when(jnp.logical_and(j == tiles_per_stream - 1, s < n_tiles))
    def _():
        for h in range(HG_HEADS):
            sto_ref[0, h] = st_s[h].T


def _sample_kernel(x_ref, tab_ref, st0_ref, cmk_ref, cmv_ref, cwk_ref, cwv_ref, sinks_ref, lbp_ref, g1_ref,
                   hgg_ref, ang_ref, n2_ref, fn_ref, w_in_ref, w_out_ref, w_fi_ref, w_fo_ref,
                   y_ref, nk_ref, nv_ref, sto_ref,
                   q_s, k_s, v_s, b_s, g_s, aq_s, ohg_s, oatt_s, st_s, ks_s, vs_s, mk_s, mv_s, *, streams, t):
    for b in range(streams):
        for h in range(HG_HEADS):
            st_s[b, h] = st0_ref[b, h].T
        for src, dst in ((cmk_ref, mk_s), (cmv_ref, mv_s)):
            d0, d1 = _dup_heads(src[b])
            dst[b, 0], dst[b, 1] = d0.astype(BF16), d1.astype(BF16)
        for src, dst in ((cwk_ref, ks_s), (cwv_ref, vs_s)):
            d0, d1 = _dup_heads(src[b])
            dst[b, 0, 0:WINDOW, :], dst[b, 1, 0:WINDOW, :] = d0.astype(BF16), d1.astype(BF16)

    x = x_ref[...]
    kv = []
    _run(_project_steps(x, tab_ref[...], _forget_lower_bound(lbp_ref), g1_ref[...], w_in_ref,
                        q_s, k_s, v_s, b_s, g_s, aq_s, t, kv))
    ak, av = kv
    nk_ref[...] = ak
    nv_ref[...] = av
    for b in range(streams):
        rs = slice(b * t, (b + 1) * t)
        _store_window(ak[rs], av[rs], ks_s.at[b], vs_s.at[b], t)
        _run(_mixer_steps(q_s, k_s, v_s, b_s, st_s.at[b], ohg_s, aq_s, ks_s.at[b], vs_s.at[b], mk_s.at[b], mv_s.at[b],
                          sinks_ref, oatt_s, b * t, t, 0, WINDOW + t, 0))
    h1 = _mix_out(x, ohg_s, g_s, oatt_s, hgg_ref[...], ang_ref[...], w_out_ref)
    hn2 = _rms(h1, n2_ref[...]).astype(BF16)
    ffn_out = []
    _run(_ffn_steps(h1, hn2, w_fi_ref, w_fo_ref, [(0, D_FF)], ffn_out))
    y_ref[...] = _rms(ffn_out[0], fn_ref[...])
    for b in range(streams):
        for h in range(HG_HEADS):
            sto_ref[b, h] = st_s[b, h].T


def _rope_table(pos):
    inv = ROPE_THETA ** (-jnp.arange(0, ATT_HEAD_DIM, 2, dtype=F32) / ATT_HEAD_DIM)
    ang = pos.astype(F32)[:, None] * inv[None, :]
    ang = jnp.concatenate([ang, ang, ang, ang], axis=-1)
    first = (jnp.arange(LANES) % ATT_HEAD_DIM) < ATT_HEAD_DIM // 2
    cos, sin = jnp.cos(ang), jnp.sin(ang)
    return jnp.concatenate([cos, jnp.where(first, -sin, 0.0), jnp.where(first, 0.0, sin)], axis=-1)


def _vmem():
    return pl.BlockSpec(memory_space=pltpu.VMEM)


def _sample_scratch(streams, t):
    wide = lambda: pltpu.VMEM((streams * t, HG_WIDTH), F32)
    window = lambda: pltpu.VMEM((streams, N_KV_HEADS, WINDOW + t, LANES), BF16)
    meta = lambda: pltpu.VMEM((streams, N_KV_HEADS, N_META, LANES), BF16)
    return [wide() for _ in range(8)] + [pltpu.VMEM((streams, HG_HEADS, HG_DIM, HG_DIM), F32),
                                         window(), window(), meta(), meta()]


def _prompt_tile(seq):
    for t in (256, 128):
        if seq % t == 0:
            return t
    raise ValueError(f"sequence length {seq} must be a multiple of 128")


def kernel(x_prompt, x_sample, cache_meta_k, cache_meta_v, cache_win_k, cache_win_v, state_hgrn, meta_tokens,
           norm1, w_in, lb_param, hg_norm, attn_sinks, attn_norm, w_out, norm2, w_ffn_in, w_ffn_out, final_norm):
    B, S, D = x_prompt.shape
    Bd, T, _ = x_sample.shape
    assert norm1.shape[0] == 1 and lb_param.shape[0] == 2, "one-layer model"
    assert D == D_MODEL and T % SUB_BLOCK == 0 and T <= CHUNK and cache_win_k.shape[2] == WINDOW
    tile = _prompt_tile(S)

    w_in_b, w_out_b = w_in[0].astype(BF16), w_out[0].astype(BF16)
    w_fi_b, w_fo_b = w_ffn_in[0].astype(BF16), w_ffn_out[0].astype(BF16)
    row = lambda a: a.reshape(1, -1).astype(F32)
    g1, hgg, ang, n2, fn = row(norm1[0]), row(hg_norm[0]), row(attn_norm[0]), row(norm2[0]), row(final_norm)
    lbp = lb_param.astype(F32)
    sinks = attn_sinks[0].astype(F32)
    tab_m = _rope_table(jnp.arange(N_META))
    tab_p = _rope_table(N_META + jnp.arange(S))
    tab_s = _rope_table(N_META + PAST_LEN + jnp.arange(T))
    params = pltpu.CompilerParams(vmem_limit_bytes=VMEM_LIMIT)

    mak, mav, mk2, mv2, stm = pl.pallas_call(
        _meta_kernel,
        out_shape=(jax.ShapeDtypeStruct((N_META, KV_WIDTH), F32), jax.ShapeDtypeStruct((N_META, KV_WIDTH), F32),
                   jax.ShapeDtypeStruct((N_KV_HEADS, N_META, LANES), BF16),
                   jax.ShapeDtypeStruct((N_KV_HEADS, N_META, LANES), BF16),
                   jax.ShapeDtypeStruct((HG_HEADS, HG_DIM, HG_DIM), F32)),
        in_specs=[_vmem()] * 5, out_specs=tuple(_vmem() for _ in range(5)),
        compiler_params=params, name="meta",
    )(meta_tokens.astype(F32), tab_m, lbp, g1, w_in_b)

    smem = pl.BlockSpec(memory_space=pltpu.SMEM)
    weights_specs = [_vmem()] * 4
    nj = S // tile
    n_tiles = B * nj
    assert nj % DMA_TILES == 0
    mix_tile = lambda s: jnp.minimum(s, n_tiles - 1)
    out_tile = lambda s: jnp.maximum(s - 1, 0)
    blk = lambda t: (t // nj, (t % nj) // DMA_TILES, 0)
    wide = lambda: pltpu.VMEM((tile, HG_WIDTH), F32)
    slots = lambda w: pltpu.VMEM((2, tile, w), F32)
    y_p, wk, wv, st_p = pl.pallas_call(
        functools.partial(_prompt_kernel, tile=tile, tiles_per_stream=nj, n_tiles=n_tiles),
        grid=(n_tiles + 1,),
        out_shape=(jax.ShapeDtypeStruct((B, S, D), F32), jax.ShapeDtypeStruct((B, WINDOW, KV_WIDTH), F32),
                   jax.ShapeDtypeStruct((B, WINDOW, KV_WIDTH), F32),
                   jax.ShapeDtypeStruct((B, HG_HEADS, HG_DIM, HG_DIM), F32)),
        in_specs=[pl.BlockSpec((1, DMA_TILES * tile, D), lambda s: blk(mix_tile(s))),
                  pl.BlockSpec((DMA_TILES * tile, 3 * LANES), lambda s: blk(mix_tile(s))[1:]),
                  _vmem(), _vmem(), _vmem(), smem] + [_vmem()] * 6 + weights_specs,
        out_specs=(pl.BlockSpec((1, DMA_TILES * tile, D), lambda s: blk(out_tile(s))),
                   pl.BlockSpec((1, WINDOW, KV_WIDTH), lambda s: (mix_tile(s) // nj, 0, 0)),
                   pl.BlockSpec((1, WINDOW, KV_WIDTH), lambda s: (mix_tile(s) // nj, 0, 0)),
                   pl.BlockSpec((1, HG_HEADS, HG_DIM, HG_DIM), lambda s: (mix_tile(s) // nj, 0, 0, 0))),
        scratch_shapes=[wide() for _ in range(5)] + [
            pltpu.VMEM((HG_HEADS, HG_DIM, HG_DIM), F32),
            pltpu.VMEM((N_KV_HEADS, WINDOW + tile, LANES), BF16),
            pltpu.VMEM((N_KV_HEADS, WINDOW + tile, LANES), BF16),
            slots(D_MODEL), slots(HG_WIDTH), slots(HG_WIDTH), slots(HG_WIDTH),
            pltpu.VMEM((tile, D_MODEL), F32), pltpu.VMEM((tile, D_MODEL), BF16)],
        compiler_params=pltpu.CompilerParams(vmem_limit_bytes=VMEM_LIMIT, dimension_semantics=("arbitrary",)),
        name="prompt",
    )(x_prompt, tab_p, stm, mk2, mv2, sinks, lbp, g1, hgg, ang, n2, fn, w_in_b, w_out_b, w_fi_b, w_fo_b)

    rows = Bd * T
    y_s, nk, nv, st_s = pl.pallas_call(
        functools.partial(_sample_kernel, streams=Bd, t=T),
        out_shape=(jax.ShapeDtypeStruct((rows, D), F32), jax.ShapeDtypeStruct((rows, KV_WIDTH), F32),
                   jax.ShapeDtypeStruct((rows, KV_WIDTH), F32),
                   jax.ShapeDtypeStruct((Bd, HG_HEADS, HG_DIM, HG_DIM), F32)),
        in_specs=[_vmem()] * 7 + [smem] + [_vmem()] * 6 + weights_specs,
        out_specs=tuple(_vmem() for _ in range(4)),
        scratch_shapes=_sample_scratch(Bd, T),
        compiler_params=params, name="sample",
    )(x_sample.reshape(rows, D), jnp.tile(tab_s, (Bd, 1)), state_hgrn[0].astype(F32),
      cache_meta_k[0].reshape(Bd, N_META, KV_WIDTH), cache_meta_v[0].reshape(Bd, N_META, KV_WIDTH),
      cache_win_k[0].reshape(Bd, WINDOW, KV_WIDTH), cache_win_v[0].reshape(Bd, WINDOW, KV_WIDTH),
      sinks, lbp, g1, hgg, ang, n2, fn, w_in_b, w_out_b, w_fi_b, w_fo_b)
    y_s = y_s.reshape(Bd, T, D)

    kv5 = lambda a, n, t: a.reshape(1, n, t, N_KV_HEADS, ATT_HEAD_DIM)
    bmeta = lambda a: jnp.broadcast_to(a.reshape(1, 1, N_META, N_KV_HEADS, ATT_HEAD_DIM),
                                       (1, B, N_META, N_KV_HEADS, ATT_HEAD_DIM))
    return (y_p, y_s, bmeta(mak), bmeta(mav), kv5(wk, B, WINDOW), kv5(wv, B, WINDOW), st_p[None],
            kv5(nk, Bd, T), kv5(nv, Bd, T), st_s[None])
```

```python
import functools

import jax
import jax.numpy as jnp
from jax import lax
from jax.experimental import pallas as pl
from jax.experimental.pallas import tpu as pltpu

F32 = jnp.float32
BF16 = jnp.bfloat16

D_MODEL = 1024
CHUNK = 64
N_META = 16
HG_WIDTH = 512
HG_HEADS = 4
HG_DIM = 128
ATT_WIDTH = 512
ATT_HEAD_DIM = 64
N_KV_HEADS = 2
Q_PER_KV = 4
KV_WIDTH = 128
WINDOW = 128
PAST_LEN = 1024
ROPE_THETA = 10000.0
D_FF = 2816
EPS = 1e-6

COL_Q, COL_F, COL_I, COL_G = 0, HG_WIDTH, 2 * HG_WIDTH, 3 * HG_WIDTH
COL_AQ = 4 * HG_WIDTH
COL_AK = COL_AQ + ATT_WIDTH
COL_AV = COL_AK + KV_WIDTH

LANES = 128
PACKED_ROWS = 16
SUB_BLOCK = 32
MXU_TILE = 256
KEY_PAD = MXU_TILE
NEG_BIG = -1e30
VMEM_LIMIT = 56 * 1024 * 1024
REGION_GROUPS = (3, 3)

NT_DIMS = (((1,), (1,)), ((), ()))
TN_DIMS = (((0,), (0,)), ((), ()))


def _rms(x, g):
    return x * lax.rsqrt(jnp.mean(x * x, axis=-1, keepdims=True) + EPS) * g


def _sigmoid(x):
    return 1.0 / (1.0 + jnp.exp(-x))


def _silu(x):
    h = 0.5 * x
    return h + h * jnp.tanh(h)


def _dot(a, b):
    return jnp.dot(a, b, preferred_element_type=F32)


def _forget_lower_bound(lbp_ref):
    p = lbp_ref[...]
    m = jnp.max(p, axis=0, keepdims=True)
    e = jnp.exp(p - m)
    return e[0:1] / jnp.sum(e, axis=0, keepdims=True)


def _cumsum_blocks(x, block):
    t = x.shape[0]
    r = lax.broadcasted_iota(jnp.int32, (t, t), 0)
    c = lax.broadcasted_iota(jnp.int32, (t, t), 1)
    d = r - c
    tri = jnp.where(d >= 0, jnp.where(d <= (r & (block - 1)), 1.0, 0.0), 0.0).astype(BF16)
    a1 = x.astype(BF16)
    r1 = x - a1.astype(F32)
    a2 = r1.astype(BF16)
    a3 = (r1 - a2.astype(F32)).astype(BF16)
    return _dot(tri, a1) + _dot(tri, a2) + _dot(tri, a3)


def _rope(x, tab):
    cos, sin_a, sin_b = tab[:, 0:LANES], tab[:, LANES:2 * LANES], tab[:, 2 * LANES:3 * LANES]
    return x * cos + pltpu.roll(x, 96, 1) * sin_a + pltpu.roll(x, 32, 1) * sin_b


def _dup_heads(x):
    lo = lax.broadcasted_iota(jnp.int32, (1, LANES), 1) < ATT_HEAD_DIM
    sw = pltpu.roll(x, ATT_HEAD_DIM, 1)
    return jnp.where(lo, x, sw), jnp.where(lo, sw, x)


def _project_steps(x, tab, lb, g1, w_in_ref, q_ref, k_ref, v_ref, b_ref, g_ref, aq_ref, block, out):
    hn = _rms(x, g1).astype(BF16)
    col = lambda c0, w: _dot(hn, w_in_ref[:, c0:c0 + w])
    f = lb + (1.0 - lb) * _sigmoid(col(COL_F, HG_WIDTH))
    k_ref[...] = 1.0 - f
    b_ref[...] = jnp.log2(f)
    q_ref[...] = _silu(col(COL_Q, HG_WIDTH))
    g_ref[...] = col(COL_G, HG_WIDTH)
    v_ref[...] = col(COL_I, HG_WIDTH)
    aq = col(COL_AQ, ATT_WIDTH)
    for j in range(ATT_WIDTH // LANES):
        ls = slice(j * LANES, (j + 1) * LANES)
        aq_ref[:, ls] = _rope(aq[:, ls], tab) * (ATT_HEAD_DIM ** -0.5)
    akv = col(COL_AK, 2 * KV_WIDTH)
    out += [_rope(akv[:, 0:KV_WIDTH], tab), akv[:, KV_WIDTH:2 * KV_WIDTH]]
    yield
    b_ref[...] = _cumsum_blocks(b_ref[...], block)


def _mixer_steps(q_ref, k_ref, v_ref, b_ref, st_ref, ohg_ref, aq_ref, ks_ref, vs_ref, mk_ref, mv_ref, sinks_ref,
                 oatt_ref, r0, length, kb0, nband, first_valid):
    sub = min(SUB_BLOCK, length)
    nb = length // sub
    nq = length
    row8 = lax.broadcasted_iota(jnp.int32, (8, 1), 0)
    rowl = lax.broadcasted_iota(jnp.int32, (length, 1), 0)
    lo = lax.broadcasted_iota(jnp.int32, (1, LANES), 1) < ATT_HEAD_DIM
    heads = [slice(h * HG_DIM, (h + 1) * HG_DIM) for h in range(HG_HEADS)]
    rows = lambda ref, ls, a, n: ref[r0 + a:r0 + a + n, ls]

    o_inter, st_inc, a_off, scores = [], [], [], []
    for ls, h in zip(heads, range(HG_HEADS)):
        q, k, v, b = (rows(r, ls, 0, length) for r in (q_ref, k_ref, v_ref, b_ref))
        b_last = rows(b_ref, ls, length - 1, 1)
        o_inter.append(lax.dot_general((q * jnp.exp2(b)).astype(BF16), st_ref[h].astype(BF16), NT_DIMS,
                                       preferred_element_type=F32))
        k_end = (k * jnp.exp2(b_last - b)).astype(BF16)
        st_inc.append(lax.dot_general(v.astype(BF16), k_end, TN_DIMS, preferred_element_type=F32))
        blocks = [jnp.zeros((sub, length), F32)]
        for i in range(1, nb):
            c_i = rows(b_ref, ls, sub * i - 1, 1)
            rs = slice(sub * i, sub * (i + 1))
            q_i = (q[rs] * jnp.exp2(b[rs] - c_i)).astype(BF16)
            k_i = jnp.where(rowl < sub * i, k * jnp.exp2(jnp.minimum(c_i - b, 0.0)), 0.0).astype(BF16)
            blocks.append(lax.dot_general(q_i, k_i, NT_DIMS, preferred_element_type=F32))
        a_off.append(blocks)
    pad = jnp.zeros((KEY_PAD - nband - N_META, LANES), BF16)
    for g in range(N_KV_HEADS):
        keys = jnp.concatenate([ks_ref[g, kb0:kb0 + nband, :], mk_ref[g], pad], axis=0)
        qs = []
        for j in range(2):
            c0 = (2 * g + j) * LANES
            qj = aq_ref[r0:r0 + nq, c0:c0 + LANES]
            qs += [jnp.where(lo, qj, 0.0), jnp.where(lo, 0.0, qj)]
        qst = jnp.concatenate(qs, axis=0).astype(BF16)
        scores.append(lax.dot_general(qst, keys, NT_DIMS, preferred_element_type=F32))
    yield

    lane = lax.broadcasted_iota(jnp.int32, (1, LANES), 1)
    a_diag = []
    for ls in heads:
        groups = []
        for g8 in range(length // 8):
            t0 = 8 * g8
            blk0 = (t0 // sub) * sub
            q8, b8 = rows(q_ref, ls, t0, 8), rows(b_ref, ls, t0, 8)
            acc = jnp.zeros((8, LANES), F32)
            for s in range(blk0, t0 + 8):
                b_s, k_s = rows(b_ref, ls, s, 1), rows(k_ref, ls, s, 1)
                w = jnp.sum(q8 * k_s * jnp.exp2(b8 - b_s), axis=-1, keepdims=True)
                acc = jnp.where(lane == s, w, acc)
            groups.append(jnp.where(lane <= t0 + row8, acc, 0.0)[:, 0:length])
        a_diag.append(jnp.concatenate(groups, axis=0))
    yield

    for ls, h in zip(heads, range(HG_HEADS)):
        b_last = rows(b_ref, ls, length - 1, 1)
        st_ref[h] = st_ref[h] * jnp.exp2(b_last) + st_inc[h]
        a = (jnp.concatenate(a_off[h], axis=0) + a_diag[h]).astype(BF16)
        ohg_ref[r0:r0 + length, ls] = o_inter[h] + _dot(a, rows(v_ref, ls, 0, length).astype(BF16))
    kidx = lax.broadcasted_iota(jnp.int32, (1, KEY_PAD), 1)
    valid = jnp.logical_and(kidx >= first_valid, kidx < nband + N_META)
    rid = lax.broadcasted_iota(jnp.int32, (Q_PER_KV * nq, 1), 0)
    for g in range(N_KV_HEADS):
        vals = jnp.concatenate([vs_ref[g, kb0:kb0 + nband, :], mv_ref[g], pad], axis=0)
        s = jnp.where(valid, scores[g], NEG_BIG)
        sink = jnp.where(rid < nq, sinks_ref[g, 0],
                         jnp.where(rid < 2 * nq, sinks_ref[g, 1],
                                   jnp.where(rid < 3 * nq, sinks_ref[g, 2], sinks_ref[g, 3])))
        m = jnp.maximum(jnp.max(s, axis=-1, keepdims=True), sink)
        p = jnp.exp(s - m)
        den = jnp.sum(p, axis=-1, keepdims=True) + jnp.exp(sink - m)
        pv = _dot(p.astype(BF16), vals) / den
        for j in range(2):
            c0 = (2 * g + j) * LANES
            oatt_ref[r0:r0 + nq, c0:c0 + LANES] = jnp.where(
                lo, pv[(2 * j) * nq:(2 * j + 1) * nq], pv[(2 * j + 1) * nq:(2 * j + 2) * nq])
    yield


def _ffn_steps(acc, hn2, w_fi_ref, w_fo_ref, bounds, out):
    for c0, c1 in bounds:
        gate = _dot(hn2, w_fi_ref[:, c0:c1])
        yield
        up = _dot(hn2, w_fi_ref[:, D_FF + c0:D_FF + c1])
        yield
        acc = acc + _dot((_silu(gate) * up).astype(BF16), w_fo_ref[c0:c1, :])
        yield
    out.append(acc)


def _run(*gens):
    for g in gens:
        for _ in g:
            pass


def _mix_out(x, ohg_ref, g_ref, oatt_ref, hgg, ang, w_out_ref):
    parts = []
    for h in range(HG_HEADS):
        ls = slice(h * HG_DIM, (h + 1) * HG_DIM)
        o = ohg_ref[:, ls]
        on = o * lax.rsqrt(jnp.mean(o * o, axis=-1, keepdims=True) + EPS)
        parts.append(on * hgg[:, ls] * _silu(g_ref[:, ls]))
    parts.append(_rms(oatt_ref[...], ang))
    mixed = _dot(jnp.concatenate(parts, axis=-1).astype(BF16), w_out_ref[...])
    return x + mixed


def _ffn_bounds(parts):
    tiles = D_FF // MXU_TILE
    cuts = [MXU_TILE * ((tiles * p) // parts) for p in range(parts + 1)]
    return list(zip(cuts[:-1], cuts[1:]))


def _store_window(ak, av, ks_ref, vs_ref, t):
    k0, k1 = _dup_heads(ak)
    v0, v1 = _dup_heads(av)
    ks_ref[0, WINDOW:WINDOW + t, :] = k0.astype(BF16)
    ks_ref[1, WINDOW:WINDOW + t, :] = k1.astype(BF16)
    vs_ref[0, WINDOW:WINDOW + t, :] = v0.astype(BF16)
    vs_ref[1, WINDOW:WINDOW + t, :] = v1.astype(BF16)


def _meta_kernel(x_ref, tab_ref, lbp_ref, g1_ref, w_in_ref,
                 mak_ref, mav_ref, mk2_ref, mv2_ref, st_ref):
    hn = _rms(x_ref[...], g1_ref[...]).astype(BF16)
    col = lambda c0, w: _dot(hn, w_in_ref[:, c0:c0 + w])
    lb = _forget_lower_bound(lbp_ref)
    f = lb + (1.0 - lb) * _sigmoid(col(COL_F, HG_WIDTH))
    b = _cumsum_blocks(jnp.log2(f), N_META)
    k_end = ((1.0 - f) * jnp.exp2(b[N_META - 1:N_META] - b)).astype(BF16)
    v16 = col(COL_I, HG_WIDTH).astype(BF16)
    for h in range(HG_HEADS):
        ls = slice(h * HG_DIM, (h + 1) * HG_DIM)
        st_ref[h] = lax.dot_general(v16[:, ls], k_end[:, ls], TN_DIMS, preferred_element_type=F32)
    ak = _rope(col(COL_AK, KV_WIDTH), tab_ref[...])
    av = col(COL_AV, KV_WIDTH)
    mak_ref[...] = ak
    mav_ref[...] = av
    k0, k1 = _dup_heads(ak)
    v0, v1 = _dup_heads(av)
    mk2_ref[0], mk2_ref[1] = k0.astype(BF16), k1.astype(BF16)
    mv2_ref[0], mv2_ref[1] = v0.astype(BF16), v1.astype(BF16)


def _prompt_kernel(x_ref, tab_ref, stm_ref, mk2_ref, mv2_ref, sinks_ref, lbp_ref, g1_ref, hgg_ref, ang_ref,
                   n2_ref, fn_ref, w_in_ref, w_out_ref, w_fi_ref, w_fo_ref,
                   y_ref, wk_ref, wv_ref, sto_ref,
                   q_s, k_s, v_s, b_s, aq_s, st_s, ks_s, vs_s, x2_s, g2_s, ohg2_s, oatt2_s, acc_s, hn2_s,
                   *, tile, tiles_per_stream, n_tiles):
    s = pl.program_id(0)
    cur = lax.rem(s, 2)
    prev = 1 - cur
    j = lax.rem(jnp.minimum(s, n_tiles - 1), tiles_per_stream)

    @pl.when(s == 0)
    def _():
        for ref in (g2_s, ohg2_s, oatt2_s):
            ref[1] = jnp.zeros((tile, HG_WIDTH), F32)
        x2_s[1] = jnp.zeros((tile, D_MODEL), F32)

    @pl.when(j == 0)
    def _():
        st_s[...] = stm_ref[...]
        ks_s[:, 0:WINDOW, :] = jnp.zeros((N_KV_HEADS, WINDOW, LANES), BF16)
        vs_s[:, 0:WINDOW, :] = jnp.zeros((N_KV_HEADS, WINDOW, LANES), BF16)

    once = jnp.minimum(s, 0) + 1
    n_chunks = tile // CHUNK

    def project_piece():
        x = x_ref[0]
        x2_s[cur] = x
        kv = []
        proj = _project_steps(x, tab_ref[...], _forget_lower_bound(lbp_ref), g1_ref[...], w_in_ref,
                              q_s, k_s, v_s, b_s, g2_s.at[cur], aq_s, CHUNK, kv)
        next(proj)
        h1 = _mix_out(x2_s[prev], ohg2_s.at[prev], g2_s.at[prev], oatt2_s.at[prev], hgg_ref[...], ang_ref[...],
                      w_out_ref)
        acc_s[...] = h1
        hn2_s[...] = _rms(h1, n2_ref[...]).astype(BF16)
        _run(proj)
        ak, av = kv
        _store_window(ak, av, ks_s, vs_s, tile)
        wk_ref[0] = ak[tile - WINDOW:tile]
        wv_ref[0] = av[tile - WINDOW:tile]

    def chunk_piece(c, bounds):
        r0 = c * CHUNK
        first_valid = jnp.maximum(WINDOW - (j * n_chunks + c) * CHUNK, 0)
        ffn_out = []
        ffn = _ffn_steps(acc_s[...], hn2_s[...], w_fi_ref, w_fo_ref, [bounds], ffn_out)
        mix = _mixer_steps(q_s, k_s, v_s, b_s, st_s, ohg2_s.at[cur], aq_s, ks_s, vs_s, mk2_ref, mv2_ref,
                           sinks_ref, oatt2_s.at[cur], r0, CHUNK, r0, WINDOW + CHUNK, first_valid)
        for gen in (ffn, mix, mix, ffn, mix, ffn):
            next(gen)
        _run(ffn, mix)
        acc_s[...] = ffn_out[0]

    def final_piece():
        y_ref[0] = _rms(acc_s[...], fn_ref[...])
        ks_s[:, 0:WINDOW, :] = ks_s[:, tile:tile + WINDOW, :]
        vs_s[:, 0:WINDOW, :] = vs_s[:, tile:tile + WINDOW, :]

    pieces = [project_piece]
    pieces += [functools.partial(chunk_piece, c, bounds) for c, bounds in enumerate(_ffn_bounds(n_chunks))]
    pieces += [final_piece]
    assert sum(REGION_GROUPS) == len(pieces)
    start = 0
    for size in REGION_GROUPS:
        group = pieces[start:start + size]
        start += size
        lax.fori_loop(0, once, lambda _, carry, group=group: ([p() for p in group], carry)[1], 0)

    @pl.when(jnp.logical_and(j == tiles_per_stream - 1, s < n_tiles))
    def _():
        for h in range(HG_HEADS):
            sto_ref[0, h] = st_s[h].T


def _sample_kernel(x_ref, tab_ref, st0_ref, cmk_ref, cmv_ref, cwk_ref, cwv_ref, sinks_ref, lbp_ref, g1_ref,
                   hgg_ref, ang_ref, n2_ref, fn_ref, w_in_ref, w_out_ref, w_fi_ref, w_fo_ref,
                   y_ref, nk_ref, nv_ref, sto_ref,
                   q_s, k_s, v_s, b_s, g_s, aq_s, ohg_s, oatt_s, st_s, ks_s, vs_s, mk_s, mv_s, *, streams, t):
    for b in range(streams):
        for h in range(HG_HEADS):
            st_s[b, h] = st0_ref[b, h].T
        for src, dst in ((cmk_ref, mk_s), (cmv_ref, mv_s)):
            d0, d1 = _dup_heads(src[b])
            dst[b, 0], dst[b, 1] = d0.astype(BF16), d1.astype(BF16)
        for src, dst in ((cwk_ref, ks_s), (cwv_ref, vs_s)):
            d0, d1 = _dup_heads(src[b])
            dst[b, 0, 0:WINDOW, :], dst[b, 1, 0:WINDOW, :] = d0.astype(BF16), d1.astype(BF16)

    x = x_ref[...]
    kv = []
    _run(_project_steps(x, tab_ref[...], _forget_lower_bound(lbp_ref), g1_ref[...], w_in_ref,
                        q_s, k_s, v_s, b_s, g_s, aq_s, t, kv))
    ak, av = kv
    nk_ref[...] = ak
    nv_ref[...] = av
    for b in range(streams):
        rs = slice(b * t, (b + 1) * t)
        _store_window(ak[rs], av[rs], ks_s.at[b], vs_s.at[b], t)
        _run(_mixer_steps(q_s, k_s, v_s, b_s, st_s.at[b], ohg_s, aq_s, ks_s.at[b], vs_s.at[b], mk_s.at[b], mv_s.at[b],
                          sinks_ref, oatt_s, b * t, t, 0, WINDOW + t, 0))
    h1 = _mix_out(x, ohg_s, g_s, oatt_s, hgg_ref[...], ang_ref[...], w_out_ref)
    hn2 = _rms(h1, n2_ref[...]).astype(BF16)
    ffn_out = []
    _run(_ffn_steps(h1, hn2, w_fi_ref, w_fo_ref, [(0, D_FF)], ffn_out))
    y_ref[...] = _rms(ffn_out[0], fn_ref[...])
    for b in range(streams):
        for h in range(HG_HEADS):
            sto_ref[b, h] = st_s[b, h].T


def _rope_table(pos):
    inv = ROPE_THETA ** (-jnp.arange(0, ATT_HEAD_DIM, 2, dtype=F32) / ATT_HEAD_DIM)
    ang = pos.astype(F32)[:, None] * inv[None, :]
    ang = jnp.concatenate([ang, ang, ang, ang], axis=-1)
    first = (jnp.arange(LANES) % ATT_HEAD_DIM) < ATT_HEAD_DIM // 2
    cos, sin = jnp.cos(ang), jnp.sin(ang)
    return jnp.concatenate([cos, jnp.where(first, -sin, 0.0), jnp.where(first, 0.0, sin)], axis=-1)


def _vmem():
    return pl.BlockSpec(memory_space=pltpu.VMEM)


def _sample_scratch(streams, t):
    wide = lambda: pltpu.VMEM((streams * t, HG_WIDTH), F32)
    window = lambda: pltpu.VMEM((streams, N_KV_HEADS, WINDOW + t, LANES), BF16)
    meta = lambda: pltpu.VMEM((streams, N_KV_HEADS, N_META, LANES), BF16)
    return [wide() for _ in range(8)] + [pltpu.VMEM((streams, HG_HEADS, HG_DIM, HG_DIM), F32),
                                         window(), window(), meta(), meta()]


def _prompt_tile(seq):
    for t in (256, 128):
        if seq % t == 0:
            return t
    raise ValueError(f"sequence length {seq} must be a multiple of 128")


def kernel(x_prompt, x_sample, cache_meta_k, cache_meta_v, cache_win_k, cache_win_v, state_hgrn, meta_tokens,
           norm1, w_in, lb_param, hg_norm, attn_sinks, attn_norm, w_out, norm2, w_ffn_in, w_ffn_out, final_norm):
    B, S, D = x_prompt.shape
    Bd, T, _ = x_sample.shape
    assert norm1.shape[0] == 1 and lb_param.shape[0] == 2, "one-layer model"
    assert D == D_MODEL and T % PACKED_ROWS == 0 and T <= CHUNK and cache_win_k.shape[2] == WINDOW
    tile = _prompt_tile(S)

    w_in_b, w_out_b = w_in[0].astype(BF16), w_out[0].astype(BF16)
    w_fi_b, w_fo_b = w_ffn_in[0].astype(BF16), w_ffn_out[0].astype(BF16)
    row = lambda a: a.reshape(1, -1).astype(F32)
    g1, hgg, ang, n2, fn = row(norm1[0]), row(hg_norm[0]), row(attn_norm[0]), row(norm2[0]), row(final_norm)
    lbp = lb_param.astype(F32)
    sinks = attn_sinks[0].astype(F32)
    tab_m = _rope_table(jnp.arange(N_META))
    tab_p = _rope_table(N_META + jnp.arange(S))
    tab_s = _rope_table(N_META + PAST_LEN + jnp.arange(T))
    params = pltpu.CompilerParams(vmem_limit_bytes=VMEM_LIMIT)

    mak, mav, mk2, mv2, stm = pl.pallas_call(
        _meta_kernel,
        out_shape=(jax.ShapeDtypeStruct((N_META, KV_WIDTH), F32), jax.ShapeDtypeStruct((N_META, KV_WIDTH), F32),
                   jax.ShapeDtypeStruct((N_KV_HEADS, N_META, LANES), BF16),
                   jax.ShapeDtypeStruct((N_KV_HEADS, N_META, LANES), BF16),
                   jax.ShapeDtypeStruct((HG_HEADS, HG_DIM, HG_DIM), F32)),
        in_specs=[_vmem()] * 5, out_specs=tuple(_vmem() for _ in range(5)),
        compiler_params=params, name="meta",
    )(meta_tokens.astype(F32), tab_m, lbp, g1, w_in_b)

    smem = pl.BlockSpec(memory_space=pltpu.SMEM)
    weights_specs = [_vmem()] * 4
    nj = S // tile
    n_tiles = B * nj
    mix_tile = lambda s: jnp.minimum(s, n_tiles - 1)
    out_tile = lambda s: jnp.maximum(s - 1, 0)
    wide = lambda: pltpu.VMEM((tile, HG_WIDTH), F32)
    slots = lambda w: pltpu.VMEM((2, tile, w), F32)
    y_p, wk, wv, st_p = pl.pallas_call(
        functools.partial(_prompt_kernel, tile=tile, tiles_per_stream=nj, n_tiles=n_tiles),
        grid=(n_tiles + 1,),
        out_shape=(jax.ShapeDtypeStruct((B, S, D), F32), jax.ShapeDtypeStruct((B, WINDOW, KV_WIDTH), F32),
                   jax.ShapeDtypeStruct((B, WINDOW, KV_WIDTH), F32),
                   jax.ShapeDtypeStruct((B, HG_HEADS, HG_DIM, HG_DIM), F32)),
        in_specs=[pl.BlockSpec((1, tile, D), lambda s: (mix_tile(s) // nj, mix_tile(s) % nj, 0)),
                  pl.BlockSpec((tile, 3 * LANES), lambda s: (mix_tile(s) % nj, 0)),
                  _vmem(), _vmem(), _vmem(), smem] + [_vmem()] * 6 + weights_specs,
        out_specs=(pl.BlockSpec((1, tile, D), lambda s: (out_tile(s) // nj, out_tile(s) % nj, 0)),
                   pl.BlockSpec((1, WINDOW, KV_WIDTH), lambda s: (mix_tile(s) // nj, 0, 0)),
                   pl.BlockSpec((1, WINDOW, KV_WIDTH), lambda s: (mix_tile(s) // nj, 0, 0)),
                   pl.BlockSpec((1, HG_HEADS, HG_DIM, HG_DIM), lambda s: (mix_tile(s) // nj, 0, 0, 0))),
        scratch_shapes=[wide() for _ in range(5)] + [
            pltpu.VMEM((HG_HEADS, HG_DIM, HG_DIM), F32),
            pltpu.VMEM((N_KV_HEADS, WINDOW + tile, LANES), BF16),
            pltpu.VMEM((N_KV_HEADS, WINDOW + tile, LANES), BF16),
            slots(D_MODEL), slots(HG_WIDTH), slots(HG_WIDTH), slots(HG_WIDTH),
            pltpu.VMEM((tile, D_MODEL), F32), pltpu.VMEM((tile, D_MODEL), BF16)],
        compiler_params=pltpu.CompilerParams(vmem_limit_bytes=VMEM_LIMIT, dimension_semantics=("arbitrary",)),
        name="prompt",
    )(x_prompt, tab_p, stm, mk2, mv2, sinks, lbp, g1, hgg, ang, n2, fn, w_in_b, w_out_b, w_fi_b, w_fo_b)

    rows = Bd * T
    y_s, nk, nv, st_s = pl.pallas_call(
        functools.partial(_sample_kernel, streams=Bd, t=T),
        out_shape=(jax.ShapeDtypeStruct((rows, D), F32), jax.ShapeDtypeStruct((rows, KV_WIDTH), F32),
                   jax.ShapeDtypeStruct((rows, KV_WIDTH), F32),
                   jax.ShapeDtypeStruct((Bd, HG_HEADS, HG_DIM, HG_DIM), F32)),
        in_specs=[_vmem()] * 7 + [smem] + [_vmem()] * 6 + weights_specs,
        out_specs=tuple(_vmem() for _ in range(4)),
        scratch_shapes=_sample_scratch(Bd, T),
        compiler_params=params, name="sample",
    )(x_sample.reshape(rows, D), jnp.tile(tab_s, (Bd, 1)), state_hgrn[0].astype(F32),
      cache_meta_k[0].reshape(Bd, N_META, KV_WIDTH), cache_meta_v[0].reshape(Bd, N_META, KV_WIDTH),
      cache_win_k[0].reshape(Bd, WINDOW, KV_WIDTH), cache_win_v[0].reshape(Bd, WINDOW, KV_WIDTH),
      sinks, lbp, g1, hgg, ang, n2, fn, w_in_b, w_out_b, w_fi_b, w_fo_b)
    y_s = y_s.reshape(Bd, T, D)

    kv5 = lambda a, n, t: a.reshape(1, n, t, N_KV_HEADS, ATT_HEAD_DIM)
    bmeta = lambda a: jnp.broadcast_to(a.reshape(1, 1, N_META, N_KV_HEADS, ATT_HEAD_DIM),
                                       (1, B, N_META, N_KV_HEADS, ATT_HEAD_DIM))
    return (y_p, y_s, bmeta(mak), bmeta(mav), kv5(wk, B, WINDOW), kv5(wv, B, WINDOW), st_p[None],
            kv5(nk, Bd, T), kv5(nv, Bd, T), st_s[None])
```

```python
import functools

import jax
import jax.numpy as jnp
from jax import lax
from jax.experimental import pallas as pl
from jax.experimental.pallas import tpu as pltpu

F32 = jnp.float32
BF16 = jnp.bfloat16

D_MODEL = 1024
CHUNK = 64
N_META = 16
HG_WIDTH = 512
HG_HEADS = 4
HG_DIM = 128
ATT_WIDTH = 512
ATT_HEAD_DIM = 64
N_KV_HEADS = 2
Q_PER_KV = 4
KV_WIDTH = 128
WINDOW = 128
PAST_LEN = 1024
ROPE_THETA = 10000.0
D_FF = 2816
EPS = 1e-6

COL_Q, COL_F, COL_I, COL_G = 0, HG_WIDTH, 2 * HG_WIDTH, 3 * HG_WIDTH
COL_AQ = 4 * HG_WIDTH
COL_AK = COL_AQ + ATT_WIDTH
COL_AV = COL_AK + KV_WIDTH

LANES = 128
PACKED_ROWS = 16
SUB_BLOCK = 32
MXU_TILE = 256
KEY_PAD = MXU_TILE
NEG_BIG = -1e30
VMEM_LIMIT = 56 * 1024 * 1024
REGION_GROUPS = (3, 3)

NT_DIMS = (((1,), (1,)), ((), ()))
TN_DIMS = (((0,), (0,)), ((), ()))


def _rms(x, g):
    return x * lax.rsqrt(jnp.mean(x * x, axis=-1, keepdims=True) + EPS) * g


def _sigmoid(x):
    return 1.0 / (1.0 + jnp.exp(-x))


def _silu(x):
    h = 0.5 * x
    return h + h * jnp.tanh(h)


def _dot(a, b):
    return jnp.dot(a, b, preferred_element_type=F32)


def _forget_lower_bound(lbp_ref):
    p = lbp_ref[...]
    m = jnp.max(p, axis=0, keepdims=True)
    e = jnp.exp(p - m)
    return e[0:1] / jnp.sum(e, axis=0, keepdims=True)


def _cumsum_blocks(x, block):
    t = x.shape[0]
    r = lax.broadcasted_iota(jnp.int32, (t, t), 0)
    c = lax.broadcasted_iota(jnp.int32, (t, t), 1)
    d = r - c
    tri = jnp.where(d >= 0, jnp.where(d <= (r & (block - 1)), 1.0, 0.0), 0.0).astype(BF16)
    a1 = x.astype(BF16)
    r1 = x - a1.astype(F32)
    a2 = r1.astype(BF16)
    a3 = (r1 - a2.astype(F32)).astype(BF16)
    return _dot(tri, a1) + _dot(tri, a2) + _dot(tri, a3)


def _rope(x, tab):
    cos, sin_a, sin_b = tab[:, 0:LANES], tab[:, LANES:2 * LANES], tab[:, 2 * LANES:3 * LANES]
    return x * cos + pltpu.roll(x, 96, 1) * sin_a + pltpu.roll(x, 32, 1) * sin_b


def _dup_heads(x):
    lo = lax.broadcasted_iota(jnp.int32, (1, LANES), 1) < ATT_HEAD_DIM
    sw = pltpu.roll(x, ATT_HEAD_DIM, 1)
    return jnp.where(lo, x, sw), jnp.where(lo, sw, x)


def _project_steps(x, tab, lb, g1, w_in_ref, q_ref, k_ref, v_ref, b_ref, g_ref, aq_ref, block, out):
    hn = _rms(x, g1).astype(BF16)
    col = lambda c0, w: _dot(hn, w_in_ref[:, c0:c0 + w])
    f = lb + (1.0 - lb) * _sigmoid(col(COL_F, HG_WIDTH))
    k_ref[...] = 1.0 - f
    b_ref[...] = jnp.log2(f)
    q_ref[...] = _silu(col(COL_Q, HG_WIDTH))
    g_ref[...] = col(COL_G, HG_WIDTH)
    v_ref[...] = col(COL_I, HG_WIDTH)
    aq = col(COL_AQ, ATT_WIDTH)
    for j in range(ATT_WIDTH // LANES):
        ls = slice(j * LANES, (j + 1) * LANES)
        aq_ref[:, ls] = _rope(aq[:, ls], tab) * (ATT_HEAD_DIM ** -0.5)
    akv = col(COL_AK, 2 * KV_WIDTH)
    out += [_rope(akv[:, 0:KV_WIDTH], tab), akv[:, KV_WIDTH:2 * KV_WIDTH]]
    yield
    b_ref[...] = _cumsum_blocks(b_ref[...], block)


def _mixer_steps(q_ref, k_ref, v_ref, b_ref, st_ref, ohg_ref, aq_ref, ks_ref, vs_ref, mk_ref, mv_ref, sinks_ref,
                 oatt_ref, r0, length, kb0, nband, first_valid):
    sub = min(SUB_BLOCK, length)
    nb = length // sub
    nq = length
    row8 = lax.broadcasted_iota(jnp.int32, (8, 1), 0)
    rowl = lax.broadcasted_iota(jnp.int32, (length, 1), 0)
    lo = lax.broadcasted_iota(jnp.int32, (1, LANES), 1) < ATT_HEAD_DIM
    heads = [slice(h * HG_DIM, (h + 1) * HG_DIM) for h in range(HG_HEADS)]
    rows = lambda ref, ls, a, n: ref[r0 + a:r0 + a + n, ls]

    o_inter, st_inc, a_off, scores = [], [], [], []
    for ls, h in zip(heads, range(HG_HEADS)):
        q, k, v, b = (rows(r, ls, 0, length) for r in (q_ref, k_ref, v_ref, b_ref))
        b_last = rows(b_ref, ls, length - 1, 1)
        o_inter.append(lax.dot_general((q * jnp.exp2(b)).astype(BF16), st_ref[h].astype(BF16), NT_DIMS,
                                       preferred_element_type=F32))
        k_end = (k * jnp.exp2(b_last - b)).astype(BF16)
        st_inc.append(lax.dot_general(v.astype(BF16), k_end, TN_DIMS, preferred_element_type=F32))
        blocks = [jnp.zeros((sub, length), F32)]
        for i in range(1, nb):
            c_i = rows(b_ref, ls, sub * i - 1, 1)
            rs = slice(sub * i, sub * (i + 1))
            q_i = (q[rs] * jnp.exp2(b[rs] - c_i)).astype(BF16)
            k_i = jnp.where(rowl < sub * i, k * jnp.exp2(jnp.minimum(c_i - b, 0.0)), 0.0).astype(BF16)
            blocks.append(lax.dot_general(q_i, k_i, NT_DIMS, preferred_element_type=F32))
        a_off.append(blocks)
    pad = jnp.zeros((KEY_PAD - nband - N_META, LANES), BF16)
    for g in range(N_KV_HEADS):
        keys = jnp.concatenate([ks_ref[g, kb0:kb0 + nband, :], mk_ref[g], pad], axis=0)
        qs = []
        for j in range(2):
            c0 = (2 * g + j) * LANES
            qj = aq_ref[r0:r0 + nq, c0:c0 + LANES]
            qs += [jnp.where(lo, qj, 0.0), jnp.where(lo, 0.0, qj)]
        qst = jnp.concatenate(qs, axis=0).astype(BF16)
        scores.append(lax.dot_general(qst, keys, NT_DIMS, preferred_element_type=F32))
    yield

    lane = lax.broadcasted_iota(jnp.int32, (1, LANES), 1)
    a_diag = []
    for ls in heads:
        groups = []
        for g8 in range(length // 8):
            t0 = 8 * g8
            blk0 = (t0 // sub) * sub
            q8, b8 = rows(q_ref, ls, t0, 8), rows(b_ref, ls, t0, 8)
            acc = jnp.zeros((8, LANES), F32)
            for s in range(blk0, t0 + 8):
                b_s, k_s = rows(b_ref, ls, s, 1), rows(k_ref, ls, s, 1)
                w = jnp.sum(q8 * k_s * jnp.exp2(b8 - b_s), axis=-1, keepdims=True)
                acc = jnp.where(lane == s, w, acc)
            groups.append(jnp.where(lane <= t0 + row8, acc, 0.0)[:, 0:length])
        a_diag.append(jnp.concatenate(groups, axis=0))
    yield

    for ls, h in zip(heads, range(HG_HEADS)):
        b_last = rows(b_ref, ls, length - 1, 1)
        st_ref[h] = st_ref[h] * jnp.exp2(b_last) + st_inc[h]
        a = (jnp.concatenate(a_off[h], axis=0) + a_diag[h]).astype(BF16)
        ohg_ref[r0:r0 + length, ls] = o_inter[h] + _dot(a, rows(v_ref, ls, 0, length).astype(BF16))
    kidx = lax.broadcasted_iota(jnp.int32, (1, KEY_PAD), 1)
    valid = jnp.logical_and(kidx >= first_valid, kidx < nband + N_META)
    rid = lax.broadcasted_iota(jnp.int32, (Q_PER_KV * nq, 1), 0)
    for g in range(N_KV_HEADS):
        vals = jnp.concatenate([vs_ref[g, kb0:kb0 + nband, :], mv_ref[g], pad], axis=0)
        s = jnp.where(valid, scores[g], NEG_BIG)
        sink = jnp.where(rid < nq, sinks_ref[g, 0],
                         jnp.where(rid < 2 * nq, sinks_ref[g, 1],
                                   jnp.where(rid < 3 * nq, sinks_ref[g, 2], sinks_ref[g, 3])))
        m = jnp.maximum(jnp.max(s, axis=-1, keepdims=True), sink)
        p = jnp.exp(s - m)
        den = jnp.sum(p, axis=-1, keepdims=True) + jnp.exp(sink - m)
        pv = _dot(p.astype(BF16), vals) / den
        for j in range(2):
            c0 = (2 * g + j) * LANES
            oatt_ref[r0:r0 + nq, c0:c0 + LANES] = jnp.where(
                lo, pv[(2 * j) * nq:(2 * j + 1) * nq], pv[(2 * j + 1) * nq:(2 * j + 2) * nq])
    yield


def _ffn_steps(acc, hn2, w_fi_ref, w_fo_ref, bounds, out):
    for c0, c1 in bounds:
        gate = _dot(hn2, w_fi_ref[:, c0:c1])
        yield
        up = _dot(hn2, w_fi_ref[:, D_FF + c0:D_FF + c1])
        yield
        acc = acc + _dot((_silu(gate) * up).astype(BF16), w_fo_ref[c0:c1, :])
        yield
    out.append(acc)


def _run(*gens):
    for g in gens:
        for _ in g:
            pass


def _mix_out(x, ohg_ref, g_ref, oatt_ref, hgg, ang, w_out_ref):
    parts = []
    for h in range(HG_HEADS):
        ls = slice(h * HG_DIM, (h + 1) * HG_DIM)
        o = ohg_ref[:, ls]
        on = o * lax.rsqrt(jnp.mean(o * o, axis=-1, keepdims=True) + EPS)
        parts.append(on * hgg[:, ls] * _silu(g_ref[:, ls]))
    parts.append(_rms(oatt_ref[...], ang))
    mixed = _dot(jnp.concatenate(parts, axis=-1).astype(BF16), w_out_ref[...])
    return x + mixed


def _ffn_bounds(parts):
    tiles = D_FF // MXU_TILE
    cuts = [MXU_TILE * ((tiles * p + parts - 1) // parts) for p in range(parts + 1)]
    return list(zip(cuts[:-1], cuts[1:]))


def _store_window(ak, av, ks_ref, vs_ref, t):
    k0, k1 = _dup_heads(ak)
    v0, v1 = _dup_heads(av)
    ks_ref[0, WINDOW:WINDOW + t, :] = k0.astype(BF16)
    ks_ref[1, WINDOW:WINDOW + t, :] = k1.astype(BF16)
    vs_ref[0, WINDOW:WINDOW + t, :] = v0.astype(BF16)
    vs_ref[1, WINDOW:WINDOW + t, :] = v1.astype(BF16)


def _meta_kernel(x_ref, tab_ref, lbp_ref, g1_ref, w_in_ref,
                 mak_ref, mav_ref, mk2_ref, mv2_ref, st_ref):
    hn = _rms(x_ref[...], g1_ref[...]).astype(BF16)
    col = lambda c0, w: _dot(hn, w_in_ref[:, c0:c0 + w])
    lb = _forget_lower_bound(lbp_ref)
    f = lb + (1.0 - lb) * _sigmoid(col(COL_F, HG_WIDTH))
    b = _cumsum_blocks(jnp.log2(f), N_META)
    k_end = ((1.0 - f) * jnp.exp2(b[N_META - 1:N_META] - b)).astype(BF16)
    v16 = col(COL_I, HG_WIDTH).astype(BF16)
    for h in range(HG_HEADS):
        ls = slice(h * HG_DIM, (h + 1) * HG_DIM)
        st_ref[h] = lax.dot_general(v16[:, ls], k_end[:, ls], TN_DIMS, preferred_element_type=F32)
    ak = _rope(col(COL_AK, KV_WIDTH), tab_ref[...])
    av = col(COL_AV, KV_WIDTH)
    mak_ref[...] = ak
    mav_ref[...] = av
    k0, k1 = _dup_heads(ak)
    v0, v1 = _dup_heads(av)
    mk2_ref[0], mk2_ref[1] = k0.astype(BF16), k1.astype(BF16)
    mv2_ref[0], mv2_ref[1] = v0.astype(BF16), v1.astype(BF16)


def _prompt_kernel(x_ref, tab_ref, stm_ref, mk2_ref, mv2_ref, sinks_ref, lbp_ref, g1_ref, hgg_ref, ang_ref,
                   n2_ref, fn_ref, w_in_ref, w_out_ref, w_fi_ref, w_fo_ref,
                   y_ref, wk_ref, wv_ref, sto_ref,
                   q_s, k_s, v_s, b_s, aq_s, st_s, ks_s, vs_s, x2_s, g2_s, ohg2_s, oatt2_s, acc_s, hn2_s,
                   *, tile, tiles_per_stream, n_tiles):
    s = pl.program_id(0)
    cur = lax.rem(s, 2)
    prev = 1 - cur
    j = lax.rem(jnp.minimum(s, n_tiles - 1), tiles_per_stream)

    @pl.when(s == 0)
    def _():
        for ref in (g2_s, ohg2_s, oatt2_s):
            ref[1] = jnp.zeros((tile, HG_WIDTH), F32)
        x2_s[1] = jnp.zeros((tile, D_MODEL), F32)

    @pl.when(j == 0)
    def _():
        st_s[...] = stm_ref[...]
        ks_s[:, 0:WINDOW, :] = jnp.zeros((N_KV_HEADS, WINDOW, LANES), BF16)
        vs_s[:, 0:WINDOW, :] = jnp.zeros((N_KV_HEADS, WINDOW, LANES), BF16)

    once = jnp.minimum(s, 0) + 1
    n_chunks = tile // CHUNK

    def project_piece():
        x = x_ref[0]
        x2_s[cur] = x
        kv = []
        proj = _project_steps(x, tab_ref[...], _forget_lower_bound(lbp_ref), g1_ref[...], w_in_ref,
                              q_s, k_s, v_s, b_s, g2_s.at[cur], aq_s, CHUNK, kv)
        next(proj)
        h1 = _mix_out(x2_s[prev], ohg2_s.at[prev], g2_s.at[prev], oatt2_s.at[prev], hgg_ref[...], ang_ref[...],
                      w_out_ref)
        acc_s[...] = h1
        hn2_s[...] = _rms(h1, n2_ref[...]).astype(BF16)
        _run(proj)
        ak, av = kv
        _store_window(ak, av, ks_s, vs_s, tile)
        wk_ref[0] = ak[tile - WINDOW:tile]
        wv_ref[0] = av[tile - WINDOW:tile]

    def chunk_piece(c, bounds):
        r0 = c * CHUNK
        first_valid = jnp.maximum(WINDOW - (j * n_chunks + c) * CHUNK, 0)
        ffn_out = []
        ffn = _ffn_steps(acc_s[...], hn2_s[...], w_fi_ref, w_fo_ref, [bounds], ffn_out)
        mix = _mixer_steps(q_s, k_s, v_s, b_s, st_s, ohg2_s.at[cur], aq_s, ks_s, vs_s, mk2_ref, mv2_ref,
                           sinks_ref, oatt2_s.at[cur], r0, CHUNK, r0, WINDOW + CHUNK, first_valid)
        for gen in (ffn, mix, mix, ffn, mix, ffn):
            next(gen)
        _run(ffn, mix)
        acc_s[...] = ffn_out[0]

    def final_piece():
        y_ref[0] = _rms(acc_s[...], fn_ref[...])
        ks_s[:, 0:WINDOW, :] = ks_s[:, tile:tile + WINDOW, :]
        vs_s[:, 0:WINDOW, :] = vs_s[:, tile:tile + WINDOW, :]

    pieces = [project_piece]
    pieces += [functools.partial(chunk_piece, c, bounds) for c, bounds in enumerate(_ffn_bounds(n_chunks))]
    pieces += [final_piece]
    assert sum(REGION_GROUPS) == len(pieces)
    start = 0
    for size in REGION_GROUPS:
        group = pieces[start:start + size]
        start += size
        lax.fori_loop(0, once, lambda _, carry, group=group: ([p() for p in group], carry)[1], 0)

    @pl.when(jnp.logical_and(j == tiles_per_stream - 1, s < n_tiles))
    def _():
        for h in range(HG_HEADS):
            sto_ref[0, h] = st_s[h].T


def _sample_kernel(x_ref, tab_ref, st0_ref, cmk_ref, cmv_ref, cwk_ref, cwv_ref, sinks_ref, lbp_ref, g1_ref,
                   hgg_ref, ang_ref, n2_ref, fn_ref, w_in_ref, w_out_ref, w_fi_ref, w_fo_ref,
                   y_ref, nk_ref, nv_ref, sto_ref,
                   q_s, k_s, v_s, b_s, g_s, aq_s, ohg_s, oatt_s, st_s, ks_s, vs_s, mk_s, mv_s, *, streams, t):
    for b in range(streams):
        for h in range(HG_HEADS):
            st_s[b, h] = st0_ref[b, h].T
        for src, dst in ((cmk_ref, mk_s), (cmv_ref, mv_s)):
            d0, d1 = _dup_heads(src[b])
            dst[b, 0], dst[b, 1] = d0.astype(BF16), d1.astype(BF16)
        for src, dst in ((cwk_ref, ks_s), (cwv_ref, vs_s)):
            d0, d1 = _dup_heads(src[b])
            dst[b, 0, 0:WINDOW, :], dst[b, 1, 0:WINDOW, :] = d0.astype(BF16), d1.astype(BF16)

    x = x_ref[...]
    kv = []
    _run(_project_steps(x, tab_ref[...], _forget_lower_bound(lbp_ref), g1_ref[...], w_in_ref,
                        q_s, k_s, v_s, b_s, g_s, aq_s, t, kv))
    ak, av = kv
    nk_ref[...] = ak
    nv_ref[...] = av
    for b in range(streams):
        rs = slice(b * t, (b + 1) * t)
        _store_window(ak[rs], av[rs], ks_s.at[b], vs_s.at[b], t)
        _run(_mixer_steps(q_s, k_s, v_s, b_s, st_s.at[b], ohg_s, aq_s, ks_s.at[b], vs_s.at[b], mk_s.at[b], mv_s.at[b],
                          sinks_ref, oatt_s, b * t, t, 0, WINDOW + t, 0))
    h1 = _mix_out(x, ohg_s, g_s, oatt_s, hgg_ref[...], ang_ref[...], w_out_ref)
    hn2 = _rms(h1, n2_ref[...]).astype(BF16)
    ffn_out = []
    _run(_ffn_steps(h1, hn2, w_fi_ref, w_fo_ref, [(0, D_FF)], ffn_out))
    y_ref[...] = _rms(ffn_out[0], fn_ref[...])
    for b in range(streams):
        for h in range(HG_HEADS):
            sto_ref[b, h] = st_s[b, h].T


def _rope_table(pos):
    inv = ROPE_THETA ** (-jnp.arange(0, ATT_HEAD_DIM, 2, dtype=F32) / ATT_HEAD_DIM)
    ang = pos.astype(F32)[:, None] * inv[None, :]
    ang = jnp.concatenate([ang, ang, ang, ang], axis=-1)
    first = (jnp.arange(LANES) % ATT_HEAD_DIM) < ATT_HEAD_DIM // 2
    cos, sin = jnp.cos(ang), jnp.sin(ang)
    return jnp.concatenate([cos, jnp.where(first, -sin, 0.0), jnp.where(first, 0.0, sin)], axis=-1)


def _vmem():
    return pl.BlockSpec(memory_space=pltpu.VMEM)


def _sample_scratch(streams, t):
    wide = lambda: pltpu.VMEM((streams * t, HG_WIDTH), F32)
    window = lambda: pltpu.VMEM((streams, N_KV_HEADS, WINDOW + t, LANES), BF16)
    meta = lambda: pltpu.VMEM((streams, N_KV_HEADS, N_META, LANES), BF16)
    return [wide() for _ in range(8)] + [pltpu.VMEM((streams, HG_HEADS, HG_DIM, HG_DIM), F32),
                                         window(), window(), meta(), meta()]


def _prompt_tile(seq):
    for t in (256, 128):
        if seq % t == 0:
            return t
    raise ValueError(f"sequence length {seq} must be a multiple of 128")


def kernel(x_prompt, x_sample, cache_meta_k, cache_meta_v, cache_win_k, cache_win_v, state_hgrn, meta_tokens,
           norm1, w_in, lb_param, hg_norm, attn_sinks, attn_norm, w_out, norm2, w_ffn_in, w_ffn_out, final_norm):
    B, S, D = x_prompt.shape
    Bd, T, _ = x_sample.shape
    assert norm1.shape[0] == 1 and lb_param.shape[0] == 2, "one-layer model"
    assert D == D_MODEL and T % PACKED_ROWS == 0 and T <= CHUNK and cache_win_k.shape[2] == WINDOW
    tile = _prompt_tile(S)

    w_in_b, w_out_b = w_in[0].astype(BF16), w_out[0].astype(BF16)
    w_fi_b, w_fo_b = w_ffn_in[0].astype(BF16), w_ffn_out[0].astype(BF16)
    row = lambda a: a.reshape(1, -1).astype(F32)
    g1, hgg, ang, n2, fn = row(norm1[0]), row(hg_norm[0]), row(attn_norm[0]), row(norm2[0]), row(final_norm)
    lbp = lb_param.astype(F32)
    sinks = attn_sinks[0].astype(F32)
    tab_m = _rope_table(jnp.arange(N_META))
    tab_p = _rope_table(N_META + jnp.arange(S))
    tab_s = _rope_table(N_META + PAST_LEN + jnp.arange(T))
    params = pltpu.CompilerParams(vmem_limit_bytes=VMEM_LIMIT)

    mak, mav, mk2, mv2, stm = pl.pallas_call(
        _meta_kernel,
        out_shape=(jax.ShapeDtypeStruct((N_META, KV_WIDTH), F32), jax.ShapeDtypeStruct((N_META, KV_WIDTH), F32),
                   jax.ShapeDtypeStruct((N_KV_HEADS, N_META, LANES), BF16),
                   jax.ShapeDtypeStruct((N_KV_HEADS, N_META, LANES), BF16),
                   jax.ShapeDtypeStruct((HG_HEADS, HG_DIM, HG_DIM), F32)),
        in_specs=[_vmem()] * 5, out_specs=tuple(_vmem() for _ in range(5)),
        compiler_params=params, name="meta",
    )(meta_tokens.astype(F32), tab_m, lbp, g1, w_in_b)

    smem = pl.BlockSpec(memory_space=pltpu.SMEM)
    weights_specs = [_vmem()] * 4
    nj = S // tile
    n_tiles = B * nj
    mix_tile = lambda s: jnp.minimum(s, n_tiles - 1)
    out_tile = lambda s: jnp.maximum(s - 1, 0)
    wide = lambda: pltpu.VMEM((tile, HG_WIDTH), F32)
    slots = lambda w: pltpu.VMEM((2, tile, w), F32)
    y_p, wk, wv, st_p = pl.pallas_call(
        functools.partial(_prompt_kernel, tile=tile, tiles_per_stream=nj, n_tiles=n_tiles),
        grid=(n_tiles + 1,),
        out_shape=(jax.ShapeDtypeStruct((B, S, D), F32), jax.ShapeDtypeStruct((B, WINDOW, KV_WIDTH), F32),
                   jax.ShapeDtypeStruct((B, WINDOW, KV_WIDTH), F32),
                   jax.ShapeDtypeStruct((B, HG_HEADS, HG_DIM, HG_DIM), F32)),
        in_specs=[pl.BlockSpec((1, tile, D), lambda s: (mix_tile(s) // nj, mix_tile(s) % nj, 0)),
                  pl.BlockSpec((tile, 3 * LANES), lambda s: (mix_tile(s) % nj, 0)),
                  _vmem(), _vmem(), _vmem(), smem] + [_vmem()] * 6 + weights_specs,
        out_specs=(pl.BlockSpec((1, tile, D), lambda s: (out_tile(s) // nj, out_tile(s) % nj, 0)),
                   pl.BlockSpec((1, WINDOW, KV_WIDTH), lambda s: (mix_tile(s) // nj, 0, 0)),
                   pl.BlockSpec((1, WINDOW, KV_WIDTH), lambda s: (mix_tile(s) // nj, 0, 0)),
                   pl.BlockSpec((1, HG_HEADS, HG_DIM, HG_DIM), lambda s: (mix_tile(s) // nj, 0, 0, 0))),
        scratch_shapes=[wide() for _ in range(5)] + [
            pltpu.VMEM((HG_HEADS, HG_DIM, HG_DIM), F32),
            pltpu.VMEM((N_KV_HEADS, WINDOW + tile, LANES), BF16),
            pltpu.VMEM((N_KV_HEADS, WINDOW + tile, LANES), BF16),
            slots(D_MODEL), slots(HG_WIDTH), slots(HG_WIDTH), slots(HG_WIDTH),
            pltpu.VMEM((tile, D_MODEL), F32), pltpu.VMEM((tile, D_MODEL), BF16)],
        compiler_params=pltpu.CompilerParams(vmem_limit_bytes=VMEM_LIMIT, dimension_semantics=("arbitrary",)),
        name="prompt",
    )(x_prompt, tab_p, stm, mk2, mv2, sinks, lbp, g1, hgg, ang, n2, fn, w_in_b, w_out_b, w_fi_b, w_fo_b)

    rows = Bd * T
    y_s, nk, nv, st_s = pl.pallas_call(
        functools.partial(_sample_kernel, streams=Bd, t=T),
        out_shape=(jax.ShapeDtypeStruct((rows, D), F32), jax.ShapeDtypeStruct((rows, KV_WIDTH), F32),
                   jax.ShapeDtypeStruct((rows, KV_WIDTH), F32),
                   jax.ShapeDtypeStruct((Bd, HG_HEADS, HG_DIM, HG_DIM), F32)),
        in_specs=[_vmem()] * 7 + [smem] + [_vmem()] * 6 + weights_specs,
        out_specs=tuple(_vmem() for _ in range(4)),
        scratch_shapes=_sample_scratch(Bd, T),
        compiler_params=params, name="sample",
    )(x_sample.reshape(rows, D), jnp.tile(tab_s, (Bd, 1)), state_hgrn[0].astype(F32),
      cache_meta_k[0].reshape(Bd, N_META, KV_WIDTH), cache_meta_v[0].reshape(Bd, N_META, KV_WIDTH),
      cache_win_k[0].reshape(Bd, WINDOW, KV_WIDTH), cache_win_v[0].reshape(Bd, WINDOW, KV_WIDTH),
      sinks, lbp, g1, hgg, ang, n2, fn, w_in_b, w_out_b, w_fi_b, w_fo_b)
    y_s = y_s.reshape(Bd, T, D)

    kv5 = lambda a, n, t: a.reshape(1, n, t, N_KV_HEADS, ATT_HEAD_DIM)
    bmeta = lambda a: jnp.broadcast_to(a.reshape(1, 1, N_META, N_KV_HEADS, ATT_HEAD_DIM),
                                       (1, B, N_META, N_KV_HEADS, ATT_HEAD_DIM))
    return (y_p, y_s, bmeta(mak), bmeta(mav), kv5(wk, B, WINDOW), kv5(wv, B, WINDOW), st_p[None],
            kv5(nk, Bd, T), kv5(nv, Bd, T), st_s[None])
```

```python
import functools

import jax
import jax.numpy as jnp
from jax import lax
from jax.experimental import pallas as pl
from jax.experimental.pallas import tpu as pltpu

F32 = jnp.float32
BF16 = jnp.bfloat16

D_MODEL = 1024
CHUNK = 64
N_META = 16
HG_WIDTH = 512
HG_HEADS = 4
HG_DIM = 128
ATT_WIDTH = 512
ATT_HEAD_DIM = 64
N_KV_HEADS = 2
Q_PER_KV = 4
KV_WIDTH = 128
WINDOW = 128
PAST_LEN = 1024
ROPE_THETA = 10000.0
D_FF = 2816
EPS = 1e-6

COL_Q, COL_F, COL_I, COL_G = 0, HG_WIDTH, 2 * HG_WIDTH, 3 * HG_WIDTH
COL_AQ = 4 * HG_WIDTH
COL_AK = COL_AQ + ATT_WIDTH
COL_AV = COL_AK + KV_WIDTH

LANES = 128
PACKED_ROWS = 16
SUB_BLOCK = 32
MXU_TILE = 256
KEY_PAD = MXU_TILE
NEG_BIG = -1e30
VMEM_LIMIT = 56 * 1024 * 1024
REGION_GROUPS = (2, 2)

NT_DIMS = (((1,), (1,)), ((), ()))
TN_DIMS = (((0,), (0,)), ((), ()))


def _rms(x, g):
    return x * lax.rsqrt(jnp.mean(x * x, axis=-1, keepdims=True) + EPS) * g


def _sigmoid(x):
    return 1.0 / (1.0 + jnp.exp(-x))


def _silu(x):
    h = 0.5 * x
    return h + h * jnp.tanh(h)


def _dot(a, b):
    return jnp.dot(a, b, preferred_element_type=F32)


def _forget_lower_bound(lbp_ref):
    p = lbp_ref[...]
    m = jnp.max(p, axis=0, keepdims=True)
    e = jnp.exp(p - m)
    return e[0:1] / jnp.sum(e, axis=0, keepdims=True)


def _cumsum_blocks(x, block):
    t = x.shape[0]
    r = lax.broadcasted_iota(jnp.int32, (t, t), 0)
    c = lax.broadcasted_iota(jnp.int32, (t, t), 1)
    d = r - c
    tri = jnp.where(d >= 0, jnp.where(d <= (r & (block - 1)), 1.0, 0.0), 0.0).astype(BF16)
    a1 = x.astype(BF16)
    r1 = x - a1.astype(F32)
    a2 = r1.astype(BF16)
    a3 = (r1 - a2.astype(F32)).astype(BF16)
    return _dot(tri, a1) + _dot(tri, a2) + _dot(tri, a3)


def _rope(x, tab):
    cos, sin_a, sin_b = tab[:, 0:LANES], tab[:, LANES:2 * LANES], tab[:, 2 * LANES:3 * LANES]
    return x * cos + pltpu.roll(x, 96, 1) * sin_a + pltpu.roll(x, 32, 1) * sin_b


def _dup_heads(x):
    lo = lax.broadcasted_iota(jnp.int32, (1, LANES), 1) < ATT_HEAD_DIM
    sw = pltpu.roll(x, ATT_HEAD_DIM, 1)
    return jnp.where(lo, x, sw), jnp.where(lo, sw, x)


def _project_steps(x, tab, lb, g1, w_in_ref, q_ref, k_ref, v_ref, b_ref, g_ref, aq_ref, block, out):
    hn = _rms(x, g1).astype(BF16)
    col = lambda c0, w: _dot(hn, w_in_ref[:, c0:c0 + w])
    f = lb + (1.0 - lb) * _sigmoid(col(COL_F, HG_WIDTH))
    k_ref[...] = 1.0 - f
    b_ref[...] = jnp.log2(f)
    q_ref[...] = _silu(col(COL_Q, HG_WIDTH))
    g_ref[...] = col(COL_G, HG_WIDTH)
    v_ref[...] = col(COL_I, HG_WIDTH)
    aq = col(COL_AQ, ATT_WIDTH)
    for j in range(ATT_WIDTH // LANES):
        ls = slice(j * LANES, (j + 1) * LANES)
        aq_ref[:, ls] = _rope(aq[:, ls], tab) * (ATT_HEAD_DIM ** -0.5)
    akv = col(COL_AK, 2 * KV_WIDTH)
    out += [_rope(akv[:, 0:KV_WIDTH], tab), akv[:, KV_WIDTH:2 * KV_WIDTH]]
    yield
    b_ref[...] = _cumsum_blocks(b_ref[...], block)


def _mixer_steps(q_ref, k_ref, v_ref, b_ref, st_ref, ohg_ref, aq_ref, ks_ref, vs_ref, mk_ref, mv_ref, sinks_ref,
                 oatt_ref, r0, length, kb0, nband, first_valid):
    sub = min(SUB_BLOCK, length)
    nb = length // sub
    nq = length
    row8 = lax.broadcasted_iota(jnp.int32, (8, 1), 0)
    rowl = lax.broadcasted_iota(jnp.int32, (length, 1), 0)
    lo = lax.broadcasted_iota(jnp.int32, (1, LANES), 1) < ATT_HEAD_DIM
    heads = [slice(h * HG_DIM, (h + 1) * HG_DIM) for h in range(HG_HEADS)]
    rows = lambda ref, ls, a, n: ref[r0 + a:r0 + a + n, ls]

    o_inter, st_inc, a_off, scores = [], [], [], []
    for ls, h in zip(heads, range(HG_HEADS)):
        q, k, v, b = (rows(r, ls, 0, length) for r in (q_ref, k_ref, v_ref, b_ref))
        b_last = rows(b_ref, ls, length - 1, 1)
        o_inter.append(lax.dot_general((q * jnp.exp2(b)).astype(BF16), st_ref[h].astype(BF16), NT_DIMS,
                                       preferred_element_type=F32))
        k_end = (k * jnp.exp2(b_last - b)).astype(BF16)
        st_inc.append(lax.dot_general(v.astype(BF16), k_end, TN_DIMS, preferred_element_type=F32))
        blocks = [jnp.zeros((sub, length), F32)]
        for i in range(1, nb):
            c_i = rows(b_ref, ls, sub * i - 1, 1)
            rs = slice(sub * i, sub * (i + 1))
            q_i = (q[rs] * jnp.exp2(b[rs] - c_i)).astype(BF16)
            k_i = jnp.where(rowl < sub * i, k * jnp.exp2(jnp.minimum(c_i - b, 0.0)), 0.0).astype(BF16)
            blocks.append(lax.dot_general(q_i, k_i, NT_DIMS, preferred_element_type=F32))
        a_off.append(blocks)
    pad = jnp.zeros((KEY_PAD - nband - N_META, LANES), BF16)
    for g in range(N_KV_HEADS):
        keys = jnp.concatenate([ks_ref[g, kb0:kb0 + nband, :], mk_ref[g], pad], axis=0)
        qs = []
        for j in range(2):
            c0 = (2 * g + j) * LANES
            qj = aq_ref[r0:r0 + nq, c0:c0 + LANES]
            qs += [jnp.where(lo, qj, 0.0), jnp.where(lo, 0.0, qj)]
        qst = jnp.concatenate(qs, axis=0).astype(BF16)
        scores.append(lax.dot_general(qst, keys, NT_DIMS, preferred_element_type=F32))
    yield

    lane = lax.broadcasted_iota(jnp.int32, (1, LANES), 1)
    a_diag = []
    for ls in heads:
        groups = []
        for g8 in range(length // 8):
            t0 = 8 * g8
            blk0 = (t0 // sub) * sub
            q8, b8 = rows(q_ref, ls, t0, 8), rows(b_ref, ls, t0, 8)
            acc = jnp.zeros((8, LANES), F32)
            for s in range(blk0, t0 + 8):
                b_s, k_s = rows(b_ref, ls, s, 1), rows(k_ref, ls, s, 1)
                w = jnp.sum(q8 * k_s * jnp.exp2(b8 - b_s), axis=-1, keepdims=True)
                acc = jnp.where(lane == s, w, acc)
            groups.append(jnp.where(lane <= t0 + row8, acc, 0.0)[:, 0:length])
        a_diag.append(jnp.concatenate(groups, axis=0))
    yield

    for ls, h in zip(heads, range(HG_HEADS)):
        b_last = rows(b_ref, ls, length - 1, 1)
        st_ref[h] = st_ref[h] * jnp.exp2(b_last) + st_inc[h]
        a = (jnp.concatenate(a_off[h], axis=0) + a_diag[h]).astype(BF16)
        ohg_ref[r0:r0 + length, ls] = o_inter[h] + _dot(a, rows(v_ref, ls, 0, length).astype(BF16))
    kidx = lax.broadcasted_iota(jnp.int32, (1, KEY_PAD), 1)
    valid = jnp.logical_and(kidx >= first_valid, kidx < nband + N_META)
    rid = lax.broadcasted_iota(jnp.int32, (Q_PER_KV * nq, 1), 0)
    for g in range(N_KV_HEADS):
        vals = jnp.concatenate([vs_ref[g, kb0:kb0 + nband, :], mv_ref[g], pad], axis=0)
        s = jnp.where(valid, scores[g], NEG_BIG)
        sink = jnp.where(rid < nq, sinks_ref[g, 0],
                         jnp.where(rid < 2 * nq, sinks_ref[g, 1],
                                   jnp.where(rid < 3 * nq, sinks_ref[g, 2], sinks_ref[g, 3])))
        m = jnp.maximum(jnp.max(s, axis=-1, keepdims=True), sink)
        p = jnp.exp(s - m)
        den = jnp.sum(p, axis=-1, keepdims=True) + jnp.exp(sink - m)
        pv = _dot(p.astype(BF16), vals) / den
        for j in range(2):
            c0 = (2 * g + j) * LANES
            oatt_ref[r0:r0 + nq, c0:c0 + LANES] = jnp.where(
                lo, pv[(2 * j) * nq:(2 * j + 1) * nq], pv[(2 * j + 1) * nq:(2 * j + 2) * nq])
    yield


def _ffn_steps(acc, hn2, w_fi_ref, w_fo_ref, bounds, out):
    for c0, c1 in bounds:
        gate = _dot(hn2, w_fi_ref[:, c0:c1])
        yield
        up = _dot(hn2, w_fi_ref[:, D_FF + c0:D_FF + c1])
        yield
        acc = acc + _dot((_silu(gate) * up).astype(BF16), w_fo_ref[c0:c1, :])
        yield
    out.append(acc)


def _run(*gens):
    for g in gens:
        for _ in g:
            pass


def _mix_out(x, ohg_ref, g_ref, oatt_ref, hgg, ang, w_out_ref):
    parts = []
    for h in range(HG_HEADS):
        ls = slice(h * HG_DIM, (h + 1) * HG_DIM)
        o = ohg_ref[:, ls]
        on = o * lax.rsqrt(jnp.mean(o * o, axis=-1, keepdims=True) + EPS)
        parts.append(on * hgg[:, ls] * _silu(g_ref[:, ls]))
    parts.append(_rms(oatt_ref[...], ang))
    mixed = _dot(jnp.concatenate(parts, axis=-1).astype(BF16), w_out_ref[...])
    return x + mixed


def _ffn_bounds(parts):
    tiles = D_FF // MXU_TILE
    cuts = [MXU_TILE * ((tiles * p) // parts) for p in range(parts + 1)]
    return list(zip(cuts[:-1], cuts[1:]))


def _store_window(ak, av, ks_ref, vs_ref, t):
    k0, k1 = _dup_heads(ak)
    v0, v1 = _dup_heads(av)
    ks_ref[0, WINDOW:WINDOW + t, :] = k0.astype(BF16)
    ks_ref[1, WINDOW:WINDOW + t, :] = k1.astype(BF16)
    vs_ref[0, WINDOW:WINDOW + t, :] = v0.astype(BF16)
    vs_ref[1, WINDOW:WINDOW + t, :] = v1.astype(BF16)


def _meta_kernel(x_ref, tab_ref, lbp_ref, g1_ref, w_in_ref,
                 mak_ref, mav_ref, mk2_ref, mv2_ref, st_ref):
    hn = _rms(x_ref[...], g1_ref[...]).astype(BF16)
    col = lambda c0, w: _dot(hn, w_in_ref[:, c0:c0 + w])
    lb = _forget_lower_bound(lbp_ref)
    f = lb + (1.0 - lb) * _sigmoid(col(COL_F, HG_WIDTH))
    b = _cumsum_blocks(jnp.log2(f), N_META)
    k_end = ((1.0 - f) * jnp.exp2(b[N_META - 1:N_META] - b)).astype(BF16)
    v16 = col(COL_I, HG_WIDTH).astype(BF16)
    for h in range(HG_HEADS):
        ls = slice(h * HG_DIM, (h + 1) * HG_DIM)
        st_ref[h] = lax.dot_general(v16[:, ls], k_end[:, ls], TN_DIMS, preferred_element_type=F32)
    ak = _rope(col(COL_AK, KV_WIDTH), tab_ref[...])
    av = col(COL_AV, KV_WIDTH)
    mak_ref[...] = ak
    mav_ref[...] = av
    k0, k1 = _dup_heads(ak)
    v0, v1 = _dup_heads(av)
    mk2_ref[0], mk2_ref[1] = k0.astype(BF16), k1.astype(BF16)
    mv2_ref[0], mv2_ref[1] = v0.astype(BF16), v1.astype(BF16)


def _prompt_kernel(x_ref, tab_ref, stm_ref, mk2_ref, mv2_ref, sinks_ref, lbp_ref, g1_ref, hgg_ref, ang_ref,
                   n2_ref, fn_ref, w_in_ref, w_out_ref, w_fi_ref, w_fo_ref,
                   y_ref, wk_ref, wv_ref, sto_ref,
                   q_s, k_s, v_s, b_s, aq_s, st_s, ks_s, vs_s, x2_s, g2_s, ohg2_s, oatt2_s, acc_s, hn2_s,
                   *, tile, tiles_per_stream, n_tiles):
    s = pl.program_id(0)
    cur = lax.rem(s, 2)
    prev = 1 - cur
    j = lax.rem(jnp.minimum(s, n_tiles - 1), tiles_per_stream)

    @pl.when(s == 0)
    def _():
        for ref in (g2_s, ohg2_s, oatt2_s):
            ref[1] = jnp.zeros((tile, HG_WIDTH), F32)
        x2_s[1] = jnp.zeros((tile, D_MODEL), F32)

    @pl.when(j == 0)
    def _():
        st_s[...] = stm_ref[...]
        ks_s[:, 0:WINDOW, :] = jnp.zeros((N_KV_HEADS, WINDOW, LANES), BF16)
        vs_s[:, 0:WINDOW, :] = jnp.zeros((N_KV_HEADS, WINDOW, LANES), BF16)

    once = jnp.minimum(s, 0) + 1
    n_chunks = tile // CHUNK

    def project_piece():
        x = x_ref[0]
        x2_s[cur] = x
        kv = []
        proj = _project_steps(x, tab_ref[...], _forget_lower_bound(lbp_ref), g1_ref[...], w_in_ref,
                              q_s, k_s, v_s, b_s, g2_s.at[cur], aq_s, CHUNK, kv)
        next(proj)
        h1 = _mix_out(x2_s[prev], ohg2_s.at[prev], g2_s.at[prev], oatt2_s.at[prev], hgg_ref[...], ang_ref[...],
                      w_out_ref)
        acc_s[...] = h1
        hn2_s[...] = _rms(h1, n2_ref[...]).astype(BF16)
        _run(proj)
        ak, av = kv
        _store_window(ak, av, ks_s, vs_s, tile)
        wk_ref[0] = ak[tile - WINDOW:tile]
        wv_ref[0] = av[tile - WINDOW:tile]

    def chunk_piece(c, bounds):
        ffn_out = []
        ffn = _ffn_steps(acc_s[...], hn2_s[...], w_fi_ref, w_fo_ref, [bounds], ffn_out)
        mixes = []
        for cc in (c, c + 1):
            r0 = cc * CHUNK
            first_valid = jnp.maximum(WINDOW - (j * n_chunks + cc) * CHUNK, 0)
            mixes.append(_mixer_steps(q_s, k_s, v_s, b_s, st_s, ohg2_s.at[cur], aq_s, ks_s, vs_s, mk2_ref, mv2_ref,
                                      sinks_ref, oatt2_s.at[cur], r0, CHUNK, r0, WINDOW + CHUNK, first_valid))
        m0, m1 = mixes
        for gen in (ffn, m0, m0, ffn, m0, m1, m1, ffn, m1):
            next(gen)
        _run(ffn, m0, m1)
        acc_s[...] = ffn_out[0]

    def final_piece():
        y_ref[0] = _rms(acc_s[...], fn_ref[...])
        ks_s[:, 0:WINDOW, :] = ks_s[:, tile:tile + WINDOW, :]
        vs_s[:, 0:WINDOW, :] = vs_s[:, tile:tile + WINDOW, :]

    pieces = [project_piece]
    pieces += [functools.partial(chunk_piece, 2 * i, bounds) for i, bounds in enumerate(_ffn_bounds(n_chunks // 2))]
    pieces += [final_piece]
    assert sum(REGION_GROUPS) == len(pieces)
    start = 0
    for size in REGION_GROUPS:
        group = pieces[start:start + size]
        start += size
        lax.fori_loop(0, once, lambda _, carry, group=group: ([p() for p in group], carry)[1], 0)

    @pl.when(jnp.logical_and(j == tiles_per_stream - 1, s < n_tiles))
    def _():
        for h in range(HG_HEADS):
            sto_ref[0, h] = st_s[h].T


def _sample_kernel(x_ref, tab_ref, st0_ref, cmk_ref, cmv_ref, cwk_ref, cwv_ref, sinks_ref, lbp_ref, g1_ref,
                   hgg_ref, ang_ref, n2_ref, fn_ref, w_in_ref, w_out_ref, w_fi_ref, w_fo_ref,
                   y_ref, nk_ref, nv_ref, sto_ref,
                   q_s, k_s, v_s, b_s, g_s, aq_s, ohg_s, oatt_s, st_s, ks_s, vs_s, mk_s, mv_s, *, streams, t):
    for b in range(streams):
        for h in range(HG_HEADS):
            st_s[b, h] = st0_ref[b, h].T
        for src, dst in ((cmk_ref, mk_s), (cmv_ref, mv_s)):
            d0, d1 = _dup_heads(src[b])
            dst[b, 0], dst[b, 1] = d0.astype(BF16), d1.astype(BF16)
        for src, dst in ((cwk_ref, ks_s), (cwv_ref, vs_s)):
            d0, d1 = _dup_heads(src[b])
            dst[b, 0, 0:WINDOW, :], dst[b, 1, 0:WINDOW, :] = d0.astype(BF16), d1.astype(BF16)

    x = x_ref[...]
    kv = []
    _run(_project_steps(x, tab_ref[...], _forget_lower_bound(lbp_ref), g1_ref[...], w_in_ref,
                        q_s, k_s, v_s, b_s, g_s, aq_s, t, kv))
    ak, av = kv
    nk_ref[...] = ak
    nv_ref[...] = av
    for b in range(streams):
        rs = slice(b * t, (b + 1) * t)
        _store_window(ak[rs], av[rs], ks_s.at[b], vs_s.at[b], t)
        _run(_mixer_steps(q_s, k_s, v_s, b_s, st_s.at[b], ohg_s, aq_s, ks_s.at[b], vs_s.at[b], mk_s.at[b], mv_s.at[b],
                          sinks_ref, oatt_s, b * t, t, 0, WINDOW + t, 0))
    h1 = _mix_out(x, ohg_s, g_s, oatt_s, hgg_ref[...], ang_ref[...], w_out_ref)
    hn2 = _rms(h1, n2_ref[...]).astype(BF16)
    ffn_out = []
    _run(_ffn_steps(h1, hn2, w_fi_ref, w_fo_ref, [(0, D_FF)], ffn_out))
    y_ref[...] = _rms(ffn_out[0], fn_ref[...])
    for b in range(streams):
        for h in range(HG_HEADS):
            sto_ref[b, h] = st_s[b, h].T


def _rope_table(pos):
    inv = ROPE_THETA ** (-jnp.arange(0, ATT_HEAD_DIM, 2, dtype=F32) / ATT_HEAD_DIM)
    ang = pos.astype(F32)[:, None] * inv[None, :]
    ang = jnp.concatenate([ang, ang, ang, ang], axis=-1)
    first = (jnp.arange(LANES) % ATT_HEAD_DIM) < ATT_HEAD_DIM // 2
    cos, sin = jnp.cos(ang), jnp.sin(ang)
    return jnp.concatenate([cos, jnp.where(first, -sin, 0.0), jnp.where(first, 0.0, sin)], axis=-1)


def _vmem():
    return pl.BlockSpec(memory_space=pltpu.VMEM)


def _sample_scratch(streams, t):
    wide = lambda: pltpu.VMEM((streams * t, HG_WIDTH), F32)
    window = lambda: pltpu.VMEM((streams, N_KV_HEADS, WINDOW + t, LANES), BF16)
    meta = lambda: pltpu.VMEM((streams, N_KV_HEADS, N_META, LANES), BF16)
    return [wide() for _ in range(8)] + [pltpu.VMEM((streams, HG_HEADS, HG_DIM, HG_DIM), F32),
                                         window(), window(), meta(), meta()]


def _prompt_tile(seq):
    for t in (256, 128):
        if seq % t == 0:
            return t
    raise ValueError(f"sequence length {seq} must be a multiple of 128")


def kernel(x_prompt, x_sample, cache_meta_k, cache_meta_v, cache_win_k, cache_win_v, state_hgrn, meta_tokens,
           norm1, w_in, lb_param, hg_norm, attn_sinks, attn_norm, w_out, norm2, w_ffn_in, w_ffn_out, final_norm):
    B, S, D = x_prompt.shape
    Bd, T, _ = x_sample.shape
    assert norm1.shape[0] == 1 and lb_param.shape[0] == 2, "one-layer model"
    assert D == D_MODEL and T % PACKED_ROWS == 0 and T <= CHUNK and cache_win_k.shape[2] == WINDOW
    tile = _prompt_tile(S)

    w_in_b, w_out_b = w_in[0].astype(BF16), w_out[0].astype(BF16)
    w_fi_b, w_fo_b = w_ffn_in[0].astype(BF16), w_ffn_out[0].astype(BF16)
    row = lambda a: a.reshape(1, -1).astype(F32)
    g1, hgg, ang, n2, fn = row(norm1[0]), row(hg_norm[0]), row(attn_norm[0]), row(norm2[0]), row(final_norm)
    lbp = lb_param.astype(F32)
    sinks = attn_sinks[0].astype(F32)
    tab_m = _rope_table(jnp.arange(N_META))
    tab_p = _rope_table(N_META + jnp.arange(S))
    tab_s = _rope_table(N_META + PAST_LEN + jnp.arange(T))
    params = pltpu.CompilerParams(vmem_limit_bytes=VMEM_LIMIT)

    mak, mav, mk2, mv2, stm = pl.pallas_call(
        _meta_kernel,
        out_shape=(jax.ShapeDtypeStruct((N_META, KV_WIDTH), F32), jax.ShapeDtypeStruct((N_META, KV_WIDTH), F32),
                   jax.ShapeDtypeStruct((N_KV_HEADS, N_META, LANES), BF16),
                   jax.ShapeDtypeStruct((N_KV_HEADS, N_META, LANES), BF16),
                   jax.ShapeDtypeStruct((HG_HEADS, HG_DIM, HG_DIM), F32)),
        in_specs=[_vmem()] * 5, out_specs=tuple(_vmem() for _ in range(5)),
        compiler_params=params, name="meta",
    )(meta_tokens.astype(F32), tab_m, lbp, g1, w_in_b)

    smem = pl.BlockSpec(memory_space=pltpu.SMEM)
    weights_specs = [_vmem()] * 4
    nj = S // tile
    n_tiles = B * nj
    mix_tile = lambda s: jnp.minimum(s, n_tiles - 1)
    out_tile = lambda s: jnp.maximum(s - 1, 0)
    wide = lambda: pltpu.VMEM((tile, HG_WIDTH), F32)
    slots = lambda w: pltpu.VMEM((2, tile, w), F32)
    y_p, wk, wv, st_p = pl.pallas_call(
        functools.partial(_prompt_kernel, tile=tile, tiles_per_stream=nj, n_tiles=n_tiles),
        grid=(n_tiles + 1,),
        out_shape=(jax.ShapeDtypeStruct((B, S, D), F32), jax.ShapeDtypeStruct((B, WINDOW, KV_WIDTH), F32),
                   jax.ShapeDtypeStruct((B, WINDOW, KV_WIDTH), F32),
                   jax.ShapeDtypeStruct((B, HG_HEADS, HG_DIM, HG_DIM), F32)),
        in_specs=[pl.BlockSpec((1, tile, D), lambda s: (mix_tile(s) // nj, mix_tile(s) % nj, 0)),
                  pl.BlockSpec((tile, 3 * LANES), lambda s: (mix_tile(s) % nj, 0)),
                  _vmem(), _vmem(), _vmem(), smem] + [_vmem()] * 6 + weights_specs,
        out_specs=(pl.BlockSpec((1, tile, D), lambda s: (out_tile(s) // nj, out_tile(s) % nj, 0)),
                   pl.BlockSpec((1, WINDOW, KV_WIDTH), lambda s: (mix_tile(s) // nj, 0, 0)),
                   pl.BlockSpec((1, WINDOW, KV_WIDTH), lambda s: (mix_tile(s) // nj, 0, 0)),
                   pl.BlockSpec((1, HG_HEADS, HG_DIM, HG_DIM), lambda s: (mix_tile(s) // nj, 0, 0, 0))),
        scratch_shapes=[wide() for _ in range(5)] + [
            pltpu.VMEM((HG_HEADS, HG_DIM, HG_DIM), F32),
            pltpu.VMEM((N_KV_HEADS, WINDOW + tile, LANES), BF16),
            pltpu.VMEM((N_KV_HEADS, WINDOW + tile, LANES), BF16),
            slots(D_MODEL), slots(HG_WIDTH), slots(HG_WIDTH), slots(HG_WIDTH),
            pltpu.VMEM((tile, D_MODEL), F32), pltpu.VMEM((tile, D_MODEL), BF16)],
        compiler_params=pltpu.CompilerParams(vmem_limit_bytes=VMEM_LIMIT, dimension_semantics=("arbitrary",)),
        name="prompt",
    )(x_prompt, tab_p, stm, mk2, mv2, sinks, lbp, g1, hgg, ang, n2, fn, w_in_b, w_out_b, w_fi_b, w_fo_b)

    rows = Bd * T
    y_s, nk, nv, st_s = pl.pallas_call(
        functools.partial(_sample_kernel, streams=Bd, t=T),
        out_shape=(jax.ShapeDtypeStruct((rows, D), F32), jax.ShapeDtypeStruct((rows, KV_WIDTH), F32),
                   jax.ShapeDtypeStruct((rows, KV_WIDTH), F32),
                   jax.ShapeDtypeStruct((Bd, HG_HEADS, HG_DIM, HG_DIM), F32)),
        in_specs=[_vmem()] * 7 + [smem] + [_vmem()] * 6 + weights_specs,
        out_specs=tuple(_vmem() for _ in range(4)),
        scratch_shapes=_sample_scratch(Bd, T),
        compiler_params=params, name="sample",
    )(x_sample.reshape(rows, D), jnp.tile(tab_s, (Bd, 1)), state_hgrn[0].astype(F32),
      cache_meta_k[0].reshape(Bd, N_META, KV_WIDTH), cache_meta_v[0].reshape(Bd, N_META, KV_WIDTH),
      cache_win_k[0].reshape(Bd, WINDOW, KV_WIDTH), cache_win_v[0].reshape(Bd, WINDOW, KV_WIDTH),
      sinks, lbp, g1, hgg, ang, n2, fn, w_in_b, w_out_b, w_fi_b, w_fo_b)
    y_s = y_s.reshape(Bd, T, D)

    kv5 = lambda a, n, t: a.reshape(1, n, t, N_KV_HEADS, ATT_HEAD_DIM)
    bmeta = lambda a: jnp.broadcast_to(a.reshape(1, 1, N_META, N_KV_HEADS, ATT_HEAD_DIM),
                                       (1, B, N_META, N_KV_HEADS, ATT_HEAD_DIM))
    return (y_p, y_s, bmeta(mak), bmeta(mav), kv5(wk, B, WINDOW), kv5(wv, B, WINDOW), st_p[None],
            kv5(nk, Bd, T), kv5(nv, Bd, T), st_s[None])
```
